```python
import jax, jax.numpy as jnp
from jax import lax
import numpy as np

D_MODEL = 1024
BATCH = 8
SEQ = 4096
DEPTH = 2
DEC_BATCH = 1
DEC_SEQ = 16384
PAST_LEN = 128

GRID_W = 64
BLOCK = 128
RET_HEADS = 4
RET_DK = 128
RET_DV = 256
ATT_HEADS = 8
ATT_KV_HEADS = 2
ATT_DH = 128
SGU_GROUPS = 4
SGU_WIDTH = 1024
PEER_HEADS = 8
PEER_NKEYS = 128
PEER_EXPERTS = PEER_NKEYS * PEER_NKEYS
PEER_DQ = 256
PEER_TOPK = 16
PEER_TOKEN_BLOCK = 128
N_BRANCH = 3
BRANCH_W = 1024
ROPE_THETA = 10000.0
DEEPNORM_ALPHA = (2 * DEPTH) ** 0.25
DEEPNORM_BETA = (8 * DEPTH) ** -0.25
LN_EPS = 1e-5
RMS_EPS = 1e-6

RET_QK_W = RET_HEADS * RET_DK
RET_V_W = RET_HEADS * RET_DV
ATT_Q_W = ATT_HEADS * ATT_DH
ATT_KV_W = ATT_KV_HEADS * ATT_DH
GATE_W = N_BRANCH * D_MODEL
SPLITS = (RET_QK_W, RET_QK_W, RET_V_W, RET_V_W, ATT_Q_W, ATT_KV_W, ATT_KV_W, SGU_WIDTH, SGU_WIDTH, GATE_W)
IN_W = sum(SPLITS)

kernel_name = 'hybrid_retention_gqa_gmlp_peer_encoder'


def layer_norm(x, g, b):
    xf = x.astype(jnp.float32)
    mu = jnp.mean(xf, axis=-1, keepdims=True)
    var = jnp.mean(jnp.square(xf - mu), axis=-1, keepdims=True)
    return ((xf - mu) * lax.rsqrt(var + LN_EPS) * g + b).astype(x.dtype)


def rms_norm(x, g):
    xf = x.astype(jnp.float32)
    return (xf * lax.rsqrt(jnp.mean(jnp.square(xf), axis=-1, keepdims=True) + RMS_EPS) * g).astype(x.dtype)


def axial_rope_tables(seq_len):
    rows = seq_len // GRID_W
    row = jnp.repeat(jnp.arange(rows, dtype=jnp.float32), GRID_W)
    col = jnp.tile(jnp.arange(GRID_W, dtype=jnp.float32), rows)
    n_freq = ATT_DH // 4
    inv_freq = ROPE_THETA ** (-jnp.arange(n_freq, dtype=jnp.float32) / n_freq)
    ang = jnp.concatenate([row[:, None] * inv_freq, col[:, None] * inv_freq], axis=-1)
    return jnp.cos(ang), jnp.sin(ang)


def apply_rope(x, cos, sin):
    x1, x2 = jnp.split(x, 2, axis=-1)
    c = cos[:, None, :]
    s = sin[:, None, :]
    return jnp.concatenate([x1 * c - x2 * s, x1 * s + x2 * c], axis=-1).astype(x.dtype)


def retention_scan(q, k, v, log_gamma, strict):
    b, s, h, dk = q.shape
    dv = v.shape[-1]
    nc = s // BLOCK
    qc = q.reshape(b, nc, BLOCK, h, dk)
    kc = k.reshape(b, nc, BLOCK, h, dk)
    vc = v.reshape(b, nc, BLOCK, h, dv)
    pos = jnp.arange(BLOCK, dtype=jnp.float32)
    diff = pos[:, None] - pos[None, :]
    mask = (diff > 0) if strict else (diff >= 0)
    decay = jnp.where(mask[None], jnp.exp(log_gamma[:, None, None] * jnp.maximum(diff, 0.0)[None]), 0.0)
    scores = jnp.einsum('bcnhd,bcmhd->bchnm', qc, kc) * decay[None, None]
    y_inner = jnp.einsum('bchnm,bcmhe->bcnhe', scores, vc)
    zeta = jnp.exp(log_gamma[:, None] * (BLOCK - 1.0 - pos)[None])
    chunk_state = jnp.einsum('bcmhd,hm,bcmhe->bchde', kc, zeta, vc)
    chunk_decay = jnp.exp(log_gamma * BLOCK)[None, :, None, None]

    def step(carry, st):
        return carry * chunk_decay + st, carry

    init = jnp.zeros((b, h, dk, dv), jnp.float32)
    _, prev = lax.scan(step, init, jnp.moveaxis(chunk_state, 1, 0).astype(jnp.float32))
    prev = jnp.moveaxis(prev, 0, 1)
    xi = jnp.exp(log_gamma[:, None] * (pos + 1.0)[None])
    y_cross = jnp.einsum('bcnhd,bchde,hn->bcnhe', qc, prev, xi)
    return (y_inner + y_cross).reshape(b, s, h, dv)


def bidirectional_retention(q, k, v, g, decay_fwd, decay_bwd):
    b, s = q.shape[:2]
    lg_f = jax.nn.log_sigmoid(decay_fwd.astype(jnp.float32))
    lg_b = jax.nn.log_sigmoid(decay_bwd.astype(jnp.float32))
    y_f = retention_scan(q, k, v, lg_f, False)
    y_b = jnp.flip(retention_scan(jnp.flip(q, 1), jnp.flip(k, 1), jnp.flip(v, 1), lg_b, True), 1)
    y = (y_f + y_b).astype(jnp.float32)
    mu = jnp.mean(y, axis=-1, keepdims=True)
    var = jnp.mean(jnp.square(y - mu), axis=-1, keepdims=True)
    y = ((y - mu) * lax.rsqrt(var + LN_EPS)).reshape(b, s, RET_V_W)
    return (jax.nn.silu(g.astype(jnp.float32)) * y).astype(g.dtype)


def gqa_attention(q, k, v):
    b, s, hq, dh = q.shape
    grp = hq // ATT_KV_HEADS
    qb = jnp.moveaxis(q.reshape(b, s // BLOCK, BLOCK, ATT_KV_HEADS, grp, dh), 1, 0)
    scale = dh ** -0.5

    def one_block(qblk):
        sc = jnp.einsum('bqhgd,bshd->bhgqs', qblk, k).astype(jnp.float32) * scale
        p = jax.nn.softmax(sc, axis=-1).astype(v.dtype)
        return jnp.einsum('bhgqs,bshd->bqhgd', p, v)

    out = lax.map(one_block, qb)
    return jnp.moveaxis(out, 0, 1).reshape(b, s, hq * dh)


def spatial_gating(u, v, ln_g, ln_b, w_s, b_s):
    b, s, w = v.shape
    vn = layer_norm(v, ln_g, ln_b)
    vc = vn.reshape(b, s // BLOCK, BLOCK, SGU_GROUPS, w // SGU_GROUPS)
    mixed = jnp.einsum('gnm,bcmgd->bcngd', w_s, vc) + jnp.transpose(b_s)[None, None, :, :, None]
    return u * mixed.reshape(b, s, w)


def peer_ffn(x, w_q, sub_keys, u_tab, v_tab):
    b, s, d = x.shape
    t = b * s
    xt = x.reshape(t, d)
    q = (xt @ w_q).reshape(t, PEER_HEADS, 2, PEER_DQ // 2)
    sc = jnp.einsum('thpd,hpkd->thpk', q, sub_keys).astype(jnp.float32)
    s1, i1 = lax.top_k(sc[:, :, 0], PEER_TOPK)
    s2, i2 = lax.top_k(sc[:, :, 1], PEER_TOPK)
    cand_s = (s1[..., :, None] + s2[..., None, :]).reshape(t, PEER_HEADS, PEER_TOPK * PEER_TOPK)
    cand_i = (i1[..., :, None] * PEER_NKEYS + i2[..., None, :]).reshape(t, PEER_HEADS, PEER_TOPK * PEER_TOPK)
    top_s, top_pos = lax.top_k(cand_s, PEER_TOPK)
    expert_idx = jnp.take_along_axis(cand_i, top_pos, axis=-1)
    gates = jax.nn.softmax(top_s, axis=-1)
    nb = t // PEER_TOKEN_BLOCK
    hk = PEER_HEADS * PEER_TOPK
    idx_b = expert_idx.reshape(nb, PEER_TOKEN_BLOCK, hk)
    gate_b = gates.reshape(nb, PEER_TOKEN_BLOCK, hk).astype(x.dtype)
    x_b = xt.reshape(nb, PEER_TOKEN_BLOCK, d)

    def one_block(args):
        xk, ik, gk = args
        act = jax.nn.gelu(jnp.einsum('td,ted->te', xk, u_tab[ik]), approximate=False)
        return jnp.einsum('te,ted->td', gk * act, v_tab[ik])

    out = lax.map(one_block, (x_b, idx_b, gate_b))
    return out.reshape(b, s, d)


def encoder_layer(x, cos, sin, w_in, b_gate, ret_decay_fwd, ret_decay_bwd, attn_q_norm, attn_k_norm,
                  sgu_ln_g, sgu_ln_b, sgu_w, sgu_b, w_branch, w_out, ln1_g, ln1_b,
                  peer_wq, peer_keys, peer_u, peer_v, ln2_g, ln2_b):
    b, s, _ = x.shape
    h = x @ w_in
    offs = np.cumsum(SPLITS)[:-1].tolist()
    r_q, r_k, r_v, r_g, a_q, a_k, a_v, s_u, s_v, gate_logits = jnp.split(h, offs, axis=-1)
    r_q = apply_rope(r_q.reshape(b, s, RET_HEADS, RET_DK), cos, sin)
    r_k = apply_rope(r_k.reshape(b, s, RET_HEADS, RET_DK), cos, sin) * (RET_DK ** -0.5)
    r_v = r_v.reshape(b, s, RET_HEADS, RET_DV)
    y_ret = bidirectional_retention(r_q, r_k, r_v, r_g, ret_decay_fwd, ret_decay_bwd)
    a_q = apply_rope(rms_norm(a_q.reshape(b, s, ATT_HEADS, ATT_DH), attn_q_norm), cos, sin)
    a_k = apply_rope(rms_norm(a_k.reshape(b, s, ATT_KV_HEADS, ATT_DH), attn_k_norm), cos, sin)
    a_v = a_v.reshape(b, s, ATT_KV_HEADS, ATT_DH)
    y_att = gqa_attention(a_q, a_k, a_v)
    y_sgu = spatial_gating(jax.nn.gelu(s_u, approximate=False), jax.nn.gelu(s_v, approximate=False),
                           sgu_ln_g, sgu_ln_b, sgu_w, sgu_b)
    gates = jax.nn.sigmoid((gate_logits + b_gate).astype(jnp.float32)).reshape(b, s, N_BRANCH, D_MODEL)
    merged = (gates[:, :, 0] * (y_ret @ w_branch[0]) + gates[:, :, 1] * (y_att @ w_branch[1])
              + gates[:, :, 2] * (y_sgu @ w_branch[2])).astype(x.dtype)
    x = layer_norm(DEEPNORM_ALPHA * x + merged @ w_out, ln1_g, ln1_b)
    x = layer_norm(DEEPNORM_ALPHA * x + peer_ffn(x, peer_wq, peer_keys, peer_u, peer_v), ln2_g, ln2_b)
    return x


def encoder_trunk(x, w_in, b_gate, ret_decay_fwd, ret_decay_bwd, attn_q_norm, attn_k_norm,
                  sgu_ln_g, sgu_ln_b, sgu_w, sgu_b, w_branch, w_out, ln1_g, ln1_b,
                  peer_wq, peer_keys, peer_u, peer_v, ln2_g, ln2_b):
    cos, sin = axial_rope_tables(x.shape[1])
    for l in range(DEPTH):
        x = encoder_layer(x, cos, sin, w_in[l], b_gate[l], ret_decay_fwd[l], ret_decay_bwd[l],
                          attn_q_norm[l], attn_k_norm[l], sgu_ln_g[l], sgu_ln_b[l], sgu_w[l], sgu_b[l],
                          w_branch[l], w_out[l], ln1_g[l], ln1_b[l],
                          peer_wq[l], peer_keys[l], peer_u[l], peer_v[l], ln2_g[l], ln2_b[l])
    return x


def setup_inputs(seed: int = 0) -> dict:
    key = jax.random.key(seed)
    ks = jax.random.split(key, 24)
    f32 = jnp.float32
    nrm = lambda k, shape, scale: jax.random.normal(k, shape, f32) * scale
    decay_logit = jnp.log(2.0 ** (5.0 + jnp.arange(RET_HEADS, dtype=f32)) - 1.0)
    return {
        'x_prompt': nrm(ks[0], (BATCH, SEQ, D_MODEL), 1.0),
        'x_sample': nrm(ks[1], (DEC_BATCH, DEC_SEQ, D_MODEL), 1.0),
        'w_in': nrm(ks[2], (DEPTH, D_MODEL, IN_W), D_MODEL ** -0.5),
        'b_gate': nrm(ks[3], (DEPTH, GATE_W), 0.02),
        'ret_decay_fwd': decay_logit[None] + nrm(ks[4], (DEPTH, RET_HEADS), 0.1),
        'ret_decay_bwd': decay_logit[None] + nrm(ks[5], (DEPTH, RET_HEADS), 0.1),
        'attn_q_norm': 1.0 + nrm(ks[6], (DEPTH, ATT_DH), 0.02),
        'attn_k_norm': 1.0 + nrm(ks[7], (DEPTH, ATT_DH), 0.02),
        'sgu_ln_g': 1.0 + nrm(ks[8], (DEPTH, SGU_WIDTH), 0.02),
        'sgu_ln_b': nrm(ks[9], (DEPTH, SGU_WIDTH), 0.02),
        'sgu_w': nrm(ks[10], (DEPTH, SGU_GROUPS, BLOCK, BLOCK), BLOCK ** -0.5),
        'sgu_b': 1.0 + nrm(ks[11], (DEPTH, SGU_GROUPS, BLOCK), 0.02),
        'w_branch': nrm(ks[12], (DEPTH, N_BRANCH, BRANCH_W, D_MODEL), BRANCH_W ** -0.5 * DEEPNORM_BETA),
        'w_out': nrm(ks[13], (DEPTH, D_MODEL, D_MODEL), D_MODEL ** -0.5 * DEEPNORM_BETA),
        'ln1_g': 1.0 + nrm(ks[14], (DEPTH, D_MODEL), 0.02),
        'ln1_b': nrm(ks[15], (DEPTH, D_MODEL), 0.02),
        'peer_wq': nrm(ks[16], (DEPTH, D_MODEL, PEER_HEADS * PEER_DQ), D_MODEL ** -0.5),
        'peer_keys': nrm(ks[17], (DEPTH, PEER_HEADS, 2, PEER_NKEYS, PEER_DQ // 2), (PEER_DQ // 2) ** -0.5),
        'peer_u': nrm(ks[18], (DEPTH, PEER_EXPERTS, D_MODEL), D_MODEL ** -0.5),
        'peer_v': nrm(ks[19], (DEPTH, PEER_EXPERTS, D_MODEL), DEEPNORM_BETA),
        'ln2_g': 1.0 + nrm(ks[20], (DEPTH, D_MODEL), 0.02),
        'ln2_b': nrm(ks[21], (DEPTH, D_MODEL), 0.02),
    }


def reference(x_prompt, x_sample, w_in, b_gate, ret_decay_fwd, ret_decay_bwd, attn_q_norm, attn_k_norm,
              sgu_ln_g, sgu_ln_b, sgu_w, sgu_b, w_branch, w_out, ln1_g, ln1_b,
              peer_wq, peer_keys, peer_u, peer_v, ln2_g, ln2_b):
    weights = (w_in, b_gate, ret_decay_fwd, ret_decay_bwd, attn_q_norm, attn_k_norm,
               sgu_ln_g, sgu_ln_b, sgu_w, sgu_b, w_branch, w_out, ln1_g, ln1_b,
               peer_wq, peer_keys, peer_u, peer_v, ln2_g, ln2_b)
    y_prompt = encoder_trunk(x_prompt, *weights)
    y_sample = encoder_trunk(x_sample, *weights)
    return (y_prompt, y_sample)
```

```python
import functools
import math

import jax
import jax.numpy as jnp
from jax import lax
from jax.experimental import pallas as pl
from jax.experimental.pallas import tpu as pltpu

F32 = jnp.float32
BF16 = jnp.bfloat16

D_MODEL = 1024
DEPTH = 2
GRID_W = 64
CHUNK = 128
RET_HEADS = 4
RET_DK = 128
RET_DV = 256
ATT_HEADS = 8
ATT_KV_HEADS = 2
ATT_DH = 128
ATT_GROUP = ATT_HEADS // ATT_KV_HEADS
SGU_GROUPS = 4
SGU_WIDTH = 1024
SGU_GW = SGU_WIDTH // SGU_GROUPS
PEER_HEADS = 8
PEER_NKEYS = 128
PEER_DQ = 256
PEER_TOPK = 16
N_BRANCH = 3
ROPE_THETA = 10000.0
DEEPNORM_ALPHA = (2 * DEPTH) ** 0.25
LN_EPS = 1e-5
RMS_EPS = 1e-6

RET_QK_W = RET_HEADS * RET_DK
RET_V_W = RET_HEADS * RET_DV
ATT_Q_W = ATT_HEADS * ATT_DH
ATT_KV_W = ATT_KV_HEADS * ATT_DH
GATE_W = N_BRANCH * D_MODEL
SPLITS = (RET_QK_W, RET_QK_W, RET_V_W, RET_V_W, ATT_Q_W, ATT_KV_W, ATT_KV_W, SGU_WIDTH, SGU_WIDTH, GATE_W)
OFFS = tuple(int(sum(SPLITS[:i])) for i in range(len(SPLITS) + 1))
IN_W = OFFS[-1]

LANES = 128
SUBLANES = 8
VMEM_LIMIT_BYTES = 56 * 1024 * 1024

NEG_INF = float("-inf")


def _gelu(x):
    return 0.5 * x * (1.0 + lax.erf(x * (2.0 ** -0.5)))


def _rope(x, cos2, sin2):
    return x * cos2 + pltpu.roll(x, ATT_DH // 2, 1) * sin2


def _layer_norm_rows(x, g, b):
    mu = jnp.mean(x, axis=-1, keepdims=True)
    xc = x - mu
    var = jnp.mean(xc * xc, axis=-1, keepdims=True)
    return xc * lax.rsqrt(var + LN_EPS) * g + b


def _inproj_kernel(x_ref, w_ref, cos_ref, sin_ref, bgate_ref, qn_ref, kn_ref, lng_ref, lnb_ref,
                   sw_ref, sb_ref,
                   rq_ref, rk_ref, rv_ref, rg_ref, aq_ref, ak_ref, av_ref, ysgu_ref, gates_ref):
    tb = x_ref.shape[0]
    xb = x_ref[...].astype(BF16)
    cos2 = cos_ref[...]
    sin2 = sin_ref[...]

    def proj(i):
        return jnp.dot(xb, w_ref[:, OFFS[i]:OFFS[i + 1]], preferred_element_type=F32)

    h = proj(0)
    for hd in range(RET_HEADS):
        sl = slice(hd * RET_DK, (hd + 1) * RET_DK)
        rq_ref[:, sl] = _rope(h[:, sl], cos2, sin2).astype(BF16)
    h = proj(1)
    for hd in range(RET_HEADS):
        sl = slice(hd * RET_DK, (hd + 1) * RET_DK)
        rk_ref[:, sl] = (_rope(h[:, sl], cos2, sin2) * (RET_DK ** -0.5)).astype(BF16)
    rv_ref[...] = proj(2).astype(BF16)
    h = proj(3)
    rg_ref[...] = (h * jax.nn.sigmoid(h)).astype(BF16)

    h = proj(4)
    qn = qn_ref[...]
    for hd in range(ATT_HEADS):
        sl = slice(hd * ATT_DH, (hd + 1) * ATT_DH)
        t = h[:, sl]
        t = t * lax.rsqrt(jnp.mean(t * t, axis=-1, keepdims=True) + RMS_EPS) * qn
        aq_ref[:, sl] = (_rope(t, cos2, sin2) * (ATT_DH ** -0.5)).astype(BF16)
    h = proj(5)
    kn = kn_ref[...]
    for hd in range(ATT_KV_HEADS):
        sl = slice(hd * ATT_DH, (hd + 1) * ATT_DH)
        t = h[:, sl]
        t = t * lax.rsqrt(jnp.mean(t * t, axis=-1, keepdims=True) + RMS_EPS) * kn
        ak_ref[:, sl] = _rope(t, cos2, sin2).astype(BF16)
    av_ref[...] = proj(6).astype(BF16)

    u = _gelu(proj(7))
    vn = _layer_norm_rows(_gelu(proj(8)), lng_ref[...], lnb_ref[...]).astype(BF16)
    sb = sb_ref[...]
    for c in range(tb // CHUNK):
        rows = slice(c * CHUNK, (c + 1) * CHUNK)
        for g in range(SGU_GROUPS):
            cols = slice(g * SGU_GW, (g + 1) * SGU_GW)
            mixed = jnp.dot(sw_ref[g], vn[rows, cols], preferred_element_type=F32) + sb[:, g:g + 1]
            ysgu_ref[rows, cols] = (u[rows, cols] * mixed).astype(BF16)

    gates_ref[...] = jax.nn.sigmoid(proj(9) + bgate_ref[...]).astype(BF16)


def _inproj(x2d, seq, w_bf, cos2, sin2, b_gate, qn, kn, lng, lnb, sgu_w_bf, sgu_bt, *, tb=256):
    t = x2d.shape[0]
    nblk_seq = seq // tb
    const = lambda i: (0, 0)
    row = lambda i: (i, 0)
    pos = lambda i: (i % nblk_seq, 0)
    outs = [(RET_QK_W, BF16), (RET_QK_W, BF16), (RET_V_W, BF16), (RET_V_W, BF16), (ATT_Q_W, BF16),
            (ATT_KV_W, BF16), (ATT_KV_W, BF16), (SGU_WIDTH, BF16), (GATE_W, BF16)]
    return pl.pallas_call(
        _inproj_kernel,
        grid=(t // tb,),
        in_specs=[
            pl.BlockSpec((tb, D_MODEL), row),
            pl.BlockSpec((D_MODEL, IN_W), const, pipeline_mode=pl.Buffered(1)),
            pl.BlockSpec((tb, ATT_DH), pos),
            pl.BlockSpec((tb, ATT_DH), pos),
            pl.BlockSpec((1, GATE_W), const),
            pl.BlockSpec((1, ATT_DH), const),
            pl.BlockSpec((1, ATT_DH), const),
            pl.BlockSpec((1, SGU_WIDTH), const),
            pl.BlockSpec((1, SGU_WIDTH), const),
            pl.BlockSpec((SGU_GROUPS, CHUNK, CHUNK), lambda i: (0, 0, 0)),
            pl.BlockSpec((CHUNK, SGU_GROUPS), const),
        ],
        out_specs=[pl.BlockSpec((tb, w), row) for w, _ in outs],
        out_shape=[jax.ShapeDtypeStruct((t, w), d) for w, d in outs],
        compiler_params=pltpu.CompilerParams(dimension_semantics=("arbitrary",),
                                             vmem_limit_bytes=VMEM_LIMIT_BYTES),
        name="inproj",
    )(x2d, w_bf, cos2, sin2, b_gate, qn, kn, lng, lnb, sgu_w_bf, sgu_bt)


def _retention_kernel(lg_ref, q_ref, k_ref, v_ref, g_ref, o_ref, ybuf_ref, state_ref, *, ns, rb):
    hd = pl.program_id(1)
    s = pl.program_id(2)
    nchunk = rb // CHUNK
    lg_f = lg_ref[0, hd]
    lg_b = lg_ref[1, hd]

    n_i = lax.broadcasted_iota(jnp.int32, (CHUNK, CHUNK), 0)
    m_i = lax.broadcasted_iota(jnp.int32, (CHUNK, CHUNK), 1)
    col = lax.broadcasted_iota(jnp.int32, (CHUNK, 1), 0).astype(F32)

    @pl.when(jnp.logical_or(s == 0, s == ns))
    def _():
        state_ref[...] = jnp.zeros_like(state_ref)

    def chunk_step(c, lg, decay, zeta, xi):
        rows = pl.ds(c * CHUNK, CHUNK)
        q = q_ref[rows, :]
        k = k_ref[rows, :]
        v = v_ref[rows, :]
        sc = lax.dot_general(q, k, (((1,), (1,)), ((), ())), preferred_element_type=F32) * decay
        y = jnp.dot(sc.astype(BF16), v, preferred_element_type=F32)
        st = state_ref[...]
        y = y + jnp.dot(q, st.astype(BF16), preferred_element_type=F32) * xi
        vz = (v.astype(F32) * zeta).astype(BF16)
        upd = lax.dot_general(k, vz, (((0,), (0,)), ((), ())), preferred_element_type=F32)
        state_ref[...] = st * jnp.exp(lg * CHUNK) + upd
        return y

    @pl.when(s < ns)
    def _():
        diff = (m_i - n_i).astype(F32)
        decay = jnp.where(m_i > n_i, jnp.exp(lg_b * jnp.maximum(diff, 0.0)), 0.0)
        zeta = jnp.exp(lg_b * col)
        xi = jnp.exp(lg_b * (CHUNK - col))
        base = (ns - 1 - s) * rb
        for c in reversed(range(nchunk)):
            y = chunk_step(c, lg_b, decay, zeta, xi)
            ybuf_ref[pl.ds(pl.multiple_of(base + c * CHUNK, CHUNK), CHUNK), :] = y

    @pl.when(s >= ns)
    def _():
        diff = (n_i - m_i).astype(F32)
        decay = jnp.where(n_i >= m_i, jnp.exp(lg_f * jnp.maximum(diff, 0.0)), 0.0)
        zeta = jnp.exp(lg_f * (CHUNK - 1.0 - col))
        xi = jnp.exp(lg_f * (col + 1.0))
        base = (s - ns) * rb
        for c in range(nchunk):
            y = chunk_step(c, lg_f, decay, zeta, xi)
            y = y + ybuf_ref[pl.ds(pl.multiple_of(base + c * CHUNK, CHUNK), CHUNK), :]
            mu = jnp.mean(y, axis=-1, keepdims=True)
            yc = y - mu
            var = jnp.mean(yc * yc, axis=-1, keepdims=True)
            yn = yc * lax.rsqrt(var + LN_EPS)
            rows = pl.ds(c * CHUNK, CHUNK)
            o_ref[rows, :] = (g_ref[rows, :].astype(F32) * yn).astype(BF16)


def _retention(rq, rk, rv, rg, lg, batch, seq, *, rb=512):
    t = rq.shape[0]
    ns = seq // rb

    def blk(b, h, s):
        return jnp.where(s < ns, ns - 1 - s, s - ns) + b * ns

    qk_map = lambda b, h, s, lg_ref: (blk(b, h, s), h)
    out_map = lambda b, h, s, lg_ref: (jnp.maximum(s - ns, 0) + b * ns, h)
    grid_spec = pltpu.PrefetchScalarGridSpec(
        num_scalar_prefetch=1,
        grid=(batch, RET_HEADS, 2 * ns),
        in_specs=[
            pl.BlockSpec((rb, RET_DK), qk_map),
            pl.BlockSpec((rb, RET_DK), qk_map),
            pl.BlockSpec((rb, RET_DV), qk_map),
            pl.BlockSpec((rb, RET_DV), out_map),
        ],
        out_specs=pl.BlockSpec((rb, RET_DV), out_map),
        scratch_shapes=[pltpu.VMEM((seq, RET_DV), F32), pltpu.VMEM((RET_DK, RET_DV), F32)],
    )
    return pl.pallas_call(
        functools.partial(_retention_kernel, ns=ns, rb=rb),
        grid_spec=grid_spec,
        out_shape=jax.ShapeDtypeStruct((t, RET_V_W), BF16),
        compiler_params=pltpu.CompilerParams(dimension_semantics=("arbitrary", "arbitrary", "arbitrary"),
                                             vmem_limit_bytes=VMEM_LIMIT_BYTES),
        name="retention",
    )(lg, rq, rk, rv, rg)


def _attention_kernel(q_ref, k_ref, v_ref, o_ref, qs_ref, m_ref, l_ref, acc_ref, *, tq):
    ki = pl.program_id(3)

    @pl.when(ki == 0)
    def _():
        for g in range(ATT_GROUP):
            qs_ref[g * tq:(g + 1) * tq, :] = q_ref[:, g * ATT_DH:(g + 1) * ATT_DH]
        m_ref[...] = jnp.full_like(m_ref, NEG_INF)
        l_ref[...] = jnp.zeros_like(l_ref)
        acc_ref[...] = jnp.zeros_like(acc_ref)

    s = lax.dot_general(qs_ref[...], k_ref[...], (((1,), (1,)), ((), ())), preferred_element_type=F32)
    m_prev = m_ref[...]
    m_new = jnp.maximum(m_prev, jnp.max(s, axis=-1, keepdims=True))
    alpha = jnp.exp(m_prev - m_new)
    p = jnp.exp(s - m_new)
    l_ref[...] = alpha * l_ref[...] + jnp.sum(p, axis=-1, keepdims=True)
    acc_ref[...] = alpha * acc_ref[...] + jnp.dot(p.astype(BF16), v_ref[...], preferred_element_type=F32)
    m_ref[...] = m_new

    @pl.when(ki == pl.num_programs(3) - 1)
    def _():
        out = acc_ref[...] / l_ref[...]
        for g in range(ATT_GROUP):
            o_ref[:, g * ATT_DH:(g + 1) * ATT_DH] = out[g * tq:(g + 1) * tq, :].astype(BF16)


def _attention(aq, ak, av, batch, seq, *, tq=512, tk=512):
    t = aq.shape[0]
    nq = seq // tq
    nk = seq // tk
    gw = ATT_GROUP * ATT_DH
    return pl.pallas_call(
        functools.partial(_attention_kernel, tq=tq),
        grid=(batch, ATT_KV_HEADS, nq, nk),
        in_specs=[
            pl.BlockSpec((tq, gw), lambda b, g, qi, ki: (b * nq + qi, g)),
            pl.BlockSpec((tk, ATT_DH), lambda b, g, qi, ki: (b * nk + ki, g)),
            pl.BlockSpec((tk, ATT_DH), lambda b, g, qi, ki: (b * nk + ki, g)),
        ],
        out_specs=pl.BlockSpec((tq, gw), lambda b, g, qi, ki: (b * nq + qi, g)),
        out_shape=jax.ShapeDtypeStruct((t, ATT_Q_W), BF16),
        scratch_shapes=[
            pltpu.VMEM((ATT_GROUP * tq, ATT_DH), BF16),
            pltpu.VMEM((ATT_GROUP * tq, 1), F32),
            pltpu.VMEM((ATT_GROUP * tq, 1), F32),
            pltpu.VMEM((ATT_GROUP * tq, ATT_DH), F32),
        ],
        compiler_params=pltpu.CompilerParams(
            dimension_semantics=("arbitrary", "arbitrary", "arbitrary", "arbitrary"),
            vmem_limit_bytes=VMEM_LIMIT_BYTES),
        name="attention",
    )(aq, ak, av)


def _merge_kernel(x_ref, yr_ref, ya_ref, ys_ref, gates_ref, wb_ref, wo_ref, g_ref, b_ref, o_ref):
    merged = None
    for i, y_ref in enumerate((yr_ref, ya_ref, ys_ref)):
        gate = gates_ref[:, i * D_MODEL:(i + 1) * D_MODEL].astype(F32)
        term = gate * jnp.dot(y_ref[...], wb_ref[i], preferred_element_type=F32)
        merged = term if merged is None else merged + term
    proj = jnp.dot(merged.astype(BF16), wo_ref[...], preferred_element_type=F32)
    o_ref[...] = _layer_norm_rows(DEEPNORM_ALPHA * x_ref[...] + proj, g_ref[...], b_ref[...])


def _merge(x2d, y_ret, y_att, y_sgu, gates, wb_bf, wo_bf, ln_g, ln_b, *, tb=512):
    t = x2d.shape[0]
    row = lambda i: (i, 0)
    const = lambda i: (0, 0)
    return pl.pallas_call(
        _merge_kernel,
        grid=(t // tb,),
        in_specs=[
            pl.BlockSpec((tb, D_MODEL), row),
            pl.BlockSpec((tb, D_MODEL), row),
            pl.BlockSpec((tb, D_MODEL), row),
            pl.BlockSpec((tb, D_MODEL), row),
            pl.BlockSpec((tb, GATE_W), row),
            pl.BlockSpec((N_BRANCH, D_MODEL, D_MODEL), lambda i: (0, 0, 0)),
            pl.BlockSpec((D_MODEL, D_MODEL), const),
            pl.BlockSpec((1, D_MODEL), const),
            pl.BlockSpec((1, D_MODEL), const),
        ],
        out_specs=pl.BlockSpec((tb, D_MODEL), row),
        out_shape=jax.ShapeDtypeStruct((t, D_MODEL), F32),
        compiler_params=pltpu.CompilerParams(dimension_semantics=("arbitrary",),
                                             vmem_limit_bytes=VMEM_LIMIT_BYTES),
        name="merge",
    )(x2d, y_ret, y_att, y_sgu, gates, wb_bf, wo_bf, ln_g, ln_b)


def _topk_rows(v, k):
    n = v.shape[0]
    row = lax.broadcasted_iota(jnp.int32, v.shape, 0).astype(F32)
    vals, idxs = [], []
    for _ in range(k):
        m = jnp.max(v, axis=0, keepdims=True)
        idx = jnp.min(jnp.where(v == m, row, float(n)), axis=0, keepdims=True)
        vals.append(m)
        idxs.append(idx)
        v = jnp.where(row == idx, NEG_INF, v)
    return vals, idxs


def _route_kernel(x_ref, wq_ref, keys_ref, i1_ref, i2_ref, gate_ref):
    tb = x_ref.shape[0]
    q = jnp.dot(x_ref[...].astype(BF16), wq_ref[...], preferred_element_type=F32).astype(BF16)
    half = PEER_DQ // 2
    i1_rows, i2_rows, gate_rows = [], [], []
    for h in range(PEER_HEADS):
        sub = []
        for p in range(2):
            qs = q[:, (h * 2 + p) * half:(h * 2 + p + 1) * half]
            sc = lax.dot_general(keys_ref[h * 2 + p], qs, (((1,), (1,)), ((), ())),
                                 preferred_element_type=F32)
            sub.append(_topk_rows(sc, PEER_TOPK))
        (s1, i1), (s2, i2) = sub
        s2m = jnp.concatenate(s2, axis=0)
        i2m = jnp.concatenate(i2, axis=0)
        cand = jnp.concatenate([s1[a] + s2m for a in range(PEER_TOPK)], axis=0)
        top_s, top_pos = _topk_rows(cand, PEER_TOPK)
        i1m = jnp.concatenate(i1, axis=0)
        arow = lax.broadcasted_iota(jnp.int32, (PEER_TOPK, tb), 0).astype(F32)
        top_s = jnp.concatenate(top_s, axis=0)
        e = jnp.exp(top_s - top_s[0:1, :])
        gate_rows.append(e / jnp.sum(e, axis=0, keepdims=True))
        for pos in top_pos:
            a = jnp.floor(pos * (1.0 / PEER_TOPK))
            b = pos - a * PEER_TOPK
            i1_rows.append(jnp.sum(jnp.where(arow == a, i1m, 0.0), axis=0, keepdims=True))
            i2_rows.append(jnp.sum(jnp.where(arow == b, i2m, 0.0), axis=0, keepdims=True))
    i1_ref[...] = jnp.concatenate(i1_rows, axis=0).T
    i2_ref[...] = jnp.concatenate(i2_rows, axis=0).T
    gate_ref[...] = jnp.concatenate(gate_rows, axis=0).T


def _route(x1, wq_bf, keys_bf, *, tb=128):
    t = x1.shape[0]
    hk = PEER_HEADS * PEER_TOPK
    row = lambda i: (i, 0)
    return pl.pallas_call(
        _route_kernel,
        grid=(t // tb,),
        in_specs=[
            pl.BlockSpec((tb, D_MODEL), row),
            pl.BlockSpec((D_MODEL, PEER_HEADS * PEER_DQ), lambda i: (0, 0)),
            pl.BlockSpec((PEER_HEADS * 2, PEER_NKEYS, PEER_DQ // 2), lambda i: (0, 0, 0)),
        ],
        out_specs=[pl.BlockSpec((tb, hk), row)] * 3,
        out_shape=[jax.ShapeDtypeStruct((t, hk), F32)] * 3,
        compiler_params=pltpu.CompilerParams(dimension_semantics=("arbitrary",),
                                             vmem_limit_bytes=VMEM_LIMIT_BYTES),
        name="peer_route",
    )(x1, wq_bf, keys_bf)


def _peer_kernel(x_ref, i1_ref, i2_ref, gate_ref, u_ref, v_ref, g_ref, b_ref, o_ref,
                 gs_ref, xb_ref, acc_ref, *, tb, ec, pitch):
    e = pl.program_id(1)
    ncl = ec // PEER_NKEYS

    @pl.when(e == 0)
    def _():
        xb_ref[...] = x_ref[...].astype(BF16)
        acc_ref[...] = jnp.zeros_like(acc_ref)
        key_id = lax.broadcasted_iota(jnp.int32, (PEER_NKEYS, PEER_NKEYS), 0).astype(F32)

        def build(t, carry):
            i1 = jnp.broadcast_to(i1_ref[pl.ds(t, 1), :], (PEER_NKEYS, PEER_NKEYS))
            i2 = jnp.broadcast_to(i2_ref[pl.ds(t, 1), :], (PEER_NKEYS, PEER_NKEYS))
            gt = jnp.broadcast_to(gate_ref[pl.ds(t, 1), :], (PEER_NKEYS, PEER_NKEYS))
            a_t = jnp.where(i1 == key_id, gt, 0.0).astype(BF16)
            b_t = jnp.where(i2 == key_id, 1.0, 0.0).astype(BF16)
            g_tok = lax.dot_general(a_t, b_t, (((1,), (1,)), ((), ())), preferred_element_type=F32)
            gs_ref[pl.ds(t, PEER_NKEYS, stride=pitch), :] = g_tok
            return carry

        lax.fori_loop(0, tb, build, 0)

    h = lax.dot_general(xb_ref[...], u_ref[...], (((1,), (1,)), ((), ())), preferred_element_type=F32)
    act = _gelu(h)
    parts = []
    for cl in range(ncl):
        start = pl.multiple_of((e * ncl + cl) * pitch, SUBLANES)
        parts.append(act[:, cl * PEER_NKEYS:(cl + 1) * PEER_NKEYS] * gs_ref[pl.ds(start, tb), :])
    w = jnp.concatenate(parts, axis=1).astype(BF16)
    acc_ref[...] += jnp.dot(w, v_ref[...], preferred_element_type=F32)

    @pl.when(e == pl.num_programs(1) - 1)
    def _():
        o_ref[...] = _layer_norm_rows(DEEPNORM_ALPHA * x_ref[...] + acc_ref[...], g_ref[...], b_ref[...])


def _peer(x1, i1, i2, gate, u_bf, v_bf, ln_g, ln_b, *, tb=256, ec=512):
    t = x1.shape[0]
    n_exp = u_bf.shape[0]
    hk = PEER_HEADS * PEER_TOPK
    pitch = tb + SUBLANES
    row = lambda i, e: (i, 0)
    const = lambda i, e: (0, 0)
    return pl.pallas_call(
        functools.partial(_peer_kernel, tb=tb, ec=ec, pitch=pitch),
        grid=(t // tb, n_exp // ec),
        in_specs=[
            pl.BlockSpec((tb, D_MODEL), row),
            pl.BlockSpec((tb, hk), row),
            pl.BlockSpec((tb, hk), row),
            pl.BlockSpec((tb, hk), row),
            pl.BlockSpec((ec, D_MODEL), lambda i, e: (e, 0)),
            pl.BlockSpec((ec, D_MODEL), lambda i, e: (e, 0)),
            pl.BlockSpec((1, D_MODEL), const),
            pl.BlockSpec((1, D_MODEL), const),
        ],
        out_specs=pl.BlockSpec((tb, D_MODEL), row),
        out_shape=jax.ShapeDtypeStruct((t, D_MODEL), F32),
        scratch_shapes=[
            pltpu.VMEM((PEER_NKEYS * pitch, PEER_NKEYS), F32),
            pltpu.VMEM((tb, D_MODEL), BF16),
            pltpu.VMEM((tb, D_MODEL), F32),
        ],
        compiler_params=pltpu.CompilerParams(dimension_semantics=("arbitrary", "arbitrary"),
                                             vmem_limit_bytes=VMEM_LIMIT_BYTES),
        name="peer_dense",
    )(x1, i1, i2, gate, u_bf, v_bf, ln_g, ln_b)


def _rope_tables(seq):
    rows = seq // GRID_W
    row = jnp.repeat(jnp.arange(rows, dtype=F32), GRID_W)
    col = jnp.tile(jnp.arange(GRID_W, dtype=F32), rows)
    n_freq = ATT_DH // 4
    inv_freq = ROPE_THETA ** (-jnp.arange(n_freq, dtype=F32) / n_freq)
    ang = jnp.concatenate([row[:, None] * inv_freq, col[:, None] * inv_freq], axis=-1)
    cos, sin = jnp.cos(ang), jnp.sin(ang)
    return jnp.concatenate([cos, cos], axis=-1), jnp.concatenate([-sin, sin], axis=-1)


def _layer(x2d, batch, seq, cos2, sin2, p):
    rq, rk, rv, rg, aq, ak, av, y_sgu, gates = _inproj(
        x2d, seq, p["w_in"], cos2, sin2, p["b_gate"], p["qn"], p["kn"], p["sgu_ln_g"], p["sgu_ln_b"],
        p["sgu_w"], p["sgu_bt"])
    y_ret = _retention(rq, rk, rv, rg, p["lg"], batch, seq)
    y_att = _attention(aq, ak, av, batch, seq)
    x1 = _merge(x2d, y_ret, y_att, y_sgu, gates, p["w_branch"], p["w_out"], p["ln1_g"], p["ln1_b"])
    i1, i2, gate = _route(x1, p["peer_wq"], p["peer_keys"])
    return _peer(x1, i1, i2, gate, p["peer_u"], p["peer_v"], p["ln2_g"], p["ln2_b"])


def _prep_layer(l, w_in, b_gate, ret_decay_fwd, ret_decay_bwd, attn_q_norm, attn_k_norm,
                sgu_ln_g, sgu_ln_b, sgu_w, sgu_b, w_branch, w_out, ln1_g, ln1_b,
                peer_wq, peer_keys, peer_u, peer_v, ln2_g, ln2_b):
    half = PEER_DQ // 2
    return dict(
        w_in=w_in[l].astype(BF16),
        b_gate=b_gate[l].reshape(1, GATE_W),
        lg=jnp.stack([jax.nn.log_sigmoid(ret_decay_fwd[l].astype(F32)),
                      jax.nn.log_sigmoid(ret_decay_bwd[l].astype(F32))]),
        qn=attn_q_norm[l].reshape(1, ATT_DH),
        kn=attn_k_norm[l].reshape(1, ATT_DH),
        sgu_ln_g=sgu_ln_g[l].reshape(1, SGU_WIDTH),
        sgu_ln_b=sgu_ln_b[l].reshape(1, SGU_WIDTH),
        sgu_w=sgu_w[l].astype(BF16),
        sgu_bt=jnp.transpose(sgu_b[l]),
        w_branch=w_branch[l].astype(BF16),
        w_out=w_out[l].astype(BF16),
        ln1_g=ln1_g[l].reshape(1, D_MODEL),
        ln1_b=ln1_b[l].reshape(1, D_MODEL),
        peer_wq=peer_wq[l].astype(BF16),
        peer_keys=peer_keys[l].reshape(PEER_HEADS * 2, PEER_NKEYS, half).astype(BF16),
        peer_u=peer_u[l].astype(BF16),
        peer_v=peer_v[l].astype(BF16),
        ln2_g=ln2_g[l].reshape(1, D_MODEL),
        ln2_b=ln2_b[l].reshape(1, D_MODEL),
    )


def kernel(x_prompt, x_sample, w_in, b_gate, ret_decay_fwd, ret_decay_bwd, attn_q_norm, attn_k_norm,
           sgu_ln_g, sgu_ln_b, sgu_w, sgu_b, w_branch, w_out, ln1_g, ln1_b,
           peer_wq, peer_keys, peer_u, peer_v, ln2_g, ln2_b):
    weights = (w_in, b_gate, ret_decay_fwd, ret_decay_bwd, attn_q_norm, attn_k_norm,
               sgu_ln_g, sgu_ln_b, sgu_w, sgu_b, w_branch, w_out, ln1_g, ln1_b,
               peer_wq, peer_keys, peer_u, peer_v, ln2_g, ln2_b)
    layers = [_prep_layer(l, *weights) for l in range(DEPTH)]
    outs = []
    for x in (x_prompt, x_sample):
        batch, seq, _ = x.shape
        cos2, sin2 = _rope_tables(seq)
        h = x.reshape(batch * seq, D_MODEL)
        for p in layers:
            h = _layer(h, batch, seq, cos2, sin2, p)
        outs.append(h.reshape(batch, seq, D_MODEL))
    return tuple(outs)
```

```python
import functools
import math

import jax
import jax.numpy as jnp
import numpy as np
from jax import lax
from jax.experimental import pallas as pl
from jax.experimental.pallas import tpu as pltpu

F32 = jnp.float32
BF16 = jnp.bfloat16

D_MODEL = 1024
DEPTH = 2
GRID_W = 64
CHUNK = 128
RET_HEADS = 4
RET_DK = 128
RET_DV = 256
ATT_HEADS = 8
ATT_KV_HEADS = 2
ATT_DH = 128
ATT_GROUP = ATT_HEADS // ATT_KV_HEADS
SGU_GROUPS = 4
SGU_WIDTH = 1024
SGU_GW = SGU_WIDTH // SGU_GROUPS
PEER_HEADS = 8
PEER_NKEYS = 128
PEER_DQ = 256
PEER_TOPK = 16
PEER_PLANES = PEER_NKEYS // 2
N_BRANCH = 3
ROPE_THETA = 10000.0
DEEPNORM_ALPHA = (2 * DEPTH) ** 0.25
LN_EPS = 1e-5
RMS_EPS = 1e-6

RET_QK_W = RET_HEADS * RET_DK
RET_V_W = RET_HEADS * RET_DV
ATT_Q_W = ATT_HEADS * ATT_DH
ATT_KV_W = ATT_KV_HEADS * ATT_DH
GATE_W = N_BRANCH * D_MODEL
SPLITS = (RET_QK_W, RET_QK_W, RET_V_W, RET_V_W, ATT_Q_W, ATT_KV_W, ATT_KV_W, SGU_WIDTH, SGU_WIDTH, GATE_W)
OFFS = tuple(int(sum(SPLITS[:i])) for i in range(len(SPLITS) + 1))
IN_W = OFFS[-1]

LANES = 128
SUBLANES = 8
VMEM_LIMIT_BYTES = 56 * 1024 * 1024

NEG_INF = float("-inf")
ATT_Q_SCALE = ATT_DH ** -0.5 * math.log2(math.e)


def _gelu(x):
    return 0.5 * x * (1.0 + lax.erf(x * (2.0 ** -0.5)))


def _rope(x, cos2, sin2):
    return x * cos2 + pltpu.roll(x, ATT_DH // 2, 1) * sin2


def _layer_norm_rows(x, g, b):
    mu = jnp.mean(x, axis=-1, keepdims=True)
    xc = x - mu
    var = jnp.mean(xc * xc, axis=-1, keepdims=True)
    return xc * lax.rsqrt(var + LN_EPS) * g + b


def _inproj_kernel(x_ref, w_ref, cos_ref, sin_ref, bgate_ref, qn_ref, kn_ref, lng_ref, lnb_ref,
                   sw_ref, sb_ref,
                   rq_ref, rk_ref, rv_ref, rg_ref, aq_ref, ak_ref, av_ref, ysgu_ref, gates_ref):
    tb = x_ref.shape[0]
    xb = x_ref[...].astype(BF16)
    cos2 = cos_ref[...]
    sin2 = sin_ref[...]

    def proj(i):
        return jnp.dot(xb, w_ref[:, OFFS[i]:OFFS[i + 1]], preferred_element_type=F32)

    h = proj(0)
    for hd in range(RET_HEADS):
        sl = slice(hd * RET_DK, (hd + 1) * RET_DK)
        rq_ref[:, sl] = _rope(h[:, sl], cos2, sin2).astype(BF16)
    h = proj(1)
    for hd in range(RET_HEADS):
        sl = slice(hd * RET_DK, (hd + 1) * RET_DK)
        rk_ref[:, sl] = (_rope(h[:, sl], cos2, sin2) * (RET_DK ** -0.5)).astype(BF16)
    rv_ref[...] = proj(2).astype(BF16)
    h = proj(3)
    rg_ref[...] = (h * jax.nn.sigmoid(h)).astype(BF16)

    h = proj(4)
    qn = qn_ref[...]
    for hd in range(ATT_HEADS):
        sl = slice(hd * ATT_DH, (hd + 1) * ATT_DH)
        t = h[:, sl]
        t = t * lax.rsqrt(jnp.mean(t * t, axis=-1, keepdims=True) + RMS_EPS) * qn
        aq_ref[:, sl] = (_rope(t, cos2, sin2) * ATT_Q_SCALE).astype(BF16)
    h = proj(5)
    kn = kn_ref[...]
    for hd in range(ATT_KV_HEADS):
        sl = slice(hd * ATT_DH, (hd + 1) * ATT_DH)
        t = h[:, sl]
        t = t * lax.rsqrt(jnp.mean(t * t, axis=-1, keepdims=True) + RMS_EPS) * kn
        ak_ref[:, sl] = _rope(t, cos2, sin2).astype(BF16)
    av_ref[...] = proj(6).astype(BF16)

    u = _gelu(proj(7))
    vn = _layer_norm_rows(_gelu(proj(8)), lng_ref[...], lnb_ref[...]).astype(BF16)
    sb = sb_ref[...]
    for c in range(tb // CHUNK):
        rows = slice(c * CHUNK, (c + 1) * CHUNK)
        for g in range(SGU_GROUPS):
            cols = slice(g * SGU_GW, (g + 1) * SGU_GW)
            mixed = jnp.dot(sw_ref[g], vn[rows, cols], preferred_element_type=F32) + sb[:, g:g + 1]
            ysgu_ref[rows, cols] = (u[rows, cols] * mixed).astype(BF16)

    gates_ref[...] = jax.nn.sigmoid(proj(9) + bgate_ref[...]).astype(BF16)


def _inproj(x2d, seq, w_bf, cos2, sin2, b_gate, qn, kn, lng, lnb, sgu_w_bf, sgu_bt, *, tb=256):
    t = x2d.shape[0]
    nblk_seq = seq // tb
    const = lambda i: (0, 0)
    row = lambda i: (i, 0)
    pos = lambda i: (i % nblk_seq, 0)
    outs = [(RET_QK_W, BF16), (RET_QK_W, BF16), (RET_V_W, BF16), (RET_V_W, BF16), (ATT_Q_W, BF16),
            (ATT_KV_W, BF16), (ATT_KV_W, BF16), (SGU_WIDTH, BF16), (GATE_W, BF16)]
    return pl.pallas_call(
        _inproj_kernel,
        grid=(t // tb,),
        in_specs=[
            pl.BlockSpec((tb, D_MODEL), row),
            pl.BlockSpec((D_MODEL, IN_W), const, pipeline_mode=pl.Buffered(1)),
            pl.BlockSpec((tb, ATT_DH), pos),
            pl.BlockSpec((tb, ATT_DH), pos),
            pl.BlockSpec((1, GATE_W), const),
            pl.BlockSpec((1, ATT_DH), const),
            pl.BlockSpec((1, ATT_DH), const),
            pl.BlockSpec((1, SGU_WIDTH), const),
            pl.BlockSpec((1, SGU_WIDTH), const),
            pl.BlockSpec((SGU_GROUPS, CHUNK, CHUNK), lambda i: (0, 0, 0)),
            pl.BlockSpec((CHUNK, SGU_GROUPS), const),
        ],
        out_specs=[pl.BlockSpec((tb, w), row) for w, _ in outs],
        out_shape=[jax.ShapeDtypeStruct((t, w), d) for w, d in outs],
        compiler_params=pltpu.CompilerParams(dimension_semantics=("arbitrary",),
                                             vmem_limit_bytes=VMEM_LIMIT_BYTES),
        name="inproj",
    )(x2d, w_bf, cos2, sin2, b_gate, qn, kn, lng, lnb, sgu_w_bf, sgu_bt)


def _retention_kernel(lg_ref, q_ref, k_ref, v_ref, g_ref, o_ref, ybuf_ref, state_ref, *, ns, rb):
    hd = pl.program_id(1)
    s = pl.program_id(2)
    nchunk = rb // CHUNK
    lg_f = lg_ref[0, hd]
    lg_b = lg_ref[1, hd]

    n_i = lax.broadcasted_iota(jnp.int32, (CHUNK, CHUNK), 0)
    m_i = lax.broadcasted_iota(jnp.int32, (CHUNK, CHUNK), 1)
    col = lax.broadcasted_iota(jnp.int32, (CHUNK, 1), 0).astype(F32)

    @pl.when(jnp.logical_or(s == 0, s == ns))
    def _():
        state_ref[...] = jnp.zeros_like(state_ref)

    def chunk_step(c, lg, decay, zeta, xi):
        rows = pl.ds(c * CHUNK, CHUNK)
        q = q_ref[rows, :]
        k = k_ref[rows, :]
        v = v_ref[rows, :]
        sc = lax.dot_general(q, k, (((1,), (1,)), ((), ())), preferred_element_type=F32) * decay
        y = jnp.dot(sc.astype(BF16), v, preferred_element_type=F32)
        st = state_ref[...]
        y = y + jnp.dot(q, st.astype(BF16), preferred_element_type=F32) * xi
        vz = (v.astype(F32) * zeta).astype(BF16)
        upd = lax.dot_general(k, vz, (((0,), (0,)), ((), ())), preferred_element_type=F32)
        state_ref[...] = st * jnp.exp(lg * CHUNK) + upd
        return y

    @pl.when(s < ns)
    def _():
        diff = (m_i - n_i).astype(F32)
        decay = jnp.where(m_i > n_i, jnp.exp(lg_b * jnp.maximum(diff, 0.0)), 0.0)
        zeta = jnp.exp(lg_b * col)
        xi = jnp.exp(lg_b * (CHUNK - col))
        base = (ns - 1 - s) * rb
        for c in reversed(range(nchunk)):
            y = chunk_step(c, lg_b, decay, zeta, xi)
            ybuf_ref[pl.ds(pl.multiple_of(base + c * CHUNK, CHUNK), CHUNK), :] = y

    @pl.when(s >= ns)
    def _():
        diff = (n_i - m_i).astype(F32)
        decay = jnp.where(n_i >= m_i, jnp.exp(lg_f * jnp.maximum(diff, 0.0)), 0.0)
        zeta = jnp.exp(lg_f * (CHUNK - 1.0 - col))
        xi = jnp.exp(lg_f * (col + 1.0))
        base = (s - ns) * rb
        for c in range(nchunk):
            y = chunk_step(c, lg_f, decay, zeta, xi)
            y = y + ybuf_ref[pl.ds(pl.multiple_of(base + c * CHUNK, CHUNK), CHUNK), :]
            mu = jnp.mean(y, axis=-1, keepdims=True)
            yc = y - mu
            var = jnp.mean(yc * yc, axis=-1, keepdims=True)
            yn = yc * lax.rsqrt(var + LN_EPS)
            rows = pl.ds(c * CHUNK, CHUNK)
            o_ref[rows, :] = (g_ref[rows, :].astype(F32) * yn).astype(BF16)


def _retention(rq, rk, rv, rg, lg, batch, seq, *, rb=512):
    t = rq.shape[0]
    ns = seq // rb

    def blk(b, h, s):
        return jnp.where(s < ns, ns - 1 - s, s - ns) + b * ns

    qk_map = lambda b, h, s, lg_ref: (blk(b, h, s), h)
    out_map = lambda b, h, s, lg_ref: (jnp.maximum(s - ns, 0) + b * ns, h)
    grid_spec = pltpu.PrefetchScalarGridSpec(
        num_scalar_prefetch=1,
        grid=(batch, RET_HEADS, 2 * ns),
        in_specs=[
            pl.BlockSpec((rb, RET_DK), qk_map),
            pl.BlockSpec((rb, RET_DK), qk_map),
            pl.BlockSpec((rb, RET_DV), qk_map),
            pl.BlockSpec((rb, RET_DV), out_map),
        ],
        out_specs=pl.BlockSpec((rb, RET_DV), out_map),
        scratch_shapes=[pltpu.VMEM((seq, RET_DV), F32), pltpu.VMEM((RET_DK, RET_DV), F32)],
    )
    return pl.pallas_call(
        functools.partial(_retention_kernel, ns=ns, rb=rb),
        grid_spec=grid_spec,
        out_shape=jax.ShapeDtypeStruct((t, RET_V_W), BF16),
        compiler_params=pltpu.CompilerParams(dimension_semantics=("arbitrary", "arbitrary", "arbitrary"),
                                             vmem_limit_bytes=VMEM_LIMIT_BYTES),
        name="retention",
    )(lg, rq, rk, rv, rg)


def _attention_kernel(q_ref, k_ref, v_ref, o_ref, m_ref, l_ref, acc_ref, *, tq):
    ki = pl.program_id(3)

    @pl.when(ki == 0)
    def _():
        m_ref[...] = jnp.full_like(m_ref, NEG_INF)
        l_ref[...] = jnp.zeros_like(l_ref)
        acc_ref[...] = jnp.zeros_like(acc_ref)

    k = k_ref[...]
    v = v_ref[...]
    for g in range(ATT_GROUP):
        rows = slice(g * tq, (g + 1) * tq)
        q = q_ref[:, g * ATT_DH:(g + 1) * ATT_DH]
        s = lax.dot_general(q, k, (((1,), (1,)), ((), ())), preferred_element_type=F32)
        m_prev = m_ref[rows, :]
        m_new = jnp.maximum(m_prev, jnp.max(s, axis=-1, keepdims=True))
        alpha = jnp.exp2(m_prev - m_new)
        p = jnp.exp2(s - m_new)
        l_ref[rows, :] = alpha * l_ref[rows, :] + jnp.sum(p, axis=-1, keepdims=True)
        acc_ref[rows, :] = alpha * acc_ref[rows, :] + jnp.dot(p.astype(BF16), v, preferred_element_type=F32)
        m_ref[rows, :] = m_new

    @pl.when(ki == pl.num_programs(3) - 1)
    def _():
        out = acc_ref[...] / l_ref[...]
        for g in range(ATT_GROUP):
            o_ref[:, g * ATT_DH:(g + 1) * ATT_DH] = out[g * tq:(g + 1) * tq, :].astype(BF16)


def _attention(aq, ak, av, batch, seq, *, tq=256, tk=4096):
    t = aq.shape[0]
    nq = seq // tq
    nk = seq // tk
    gw = ATT_GROUP * ATT_DH
    return pl.pallas_call(
        functools.partial(_attention_kernel, tq=tq),
        grid=(batch, ATT_KV_HEADS, nq, nk),
        in_specs=[
            pl.BlockSpec((tq, gw), lambda b, g, qi, ki: (b * nq + qi, g)),
            pl.BlockSpec((tk, ATT_DH), lambda b, g, qi, ki: (b * nk + ki, g)),
            pl.BlockSpec((tk, ATT_DH), lambda b, g, qi, ki: (b * nk + ki, g)),
        ],
        out_specs=pl.BlockSpec((tq, gw), lambda b, g, qi, ki: (b * nq + qi, g)),
        out_shape=jax.ShapeDtypeStruct((t, ATT_Q_W), BF16),
        scratch_shapes=[
            pltpu.VMEM((ATT_GROUP * tq, 1), F32),
            pltpu.VMEM((ATT_GROUP * tq, 1), F32),
            pltpu.VMEM((ATT_GROUP * tq, ATT_DH), F32),
        ],
        compiler_params=pltpu.CompilerParams(
            dimension_semantics=("arbitrary", "arbitrary", "arbitrary", "arbitrary"),
            vmem_limit_bytes=VMEM_LIMIT_BYTES),
        name="attention",
    )(aq, ak, av)


def _merge_kernel(x_ref, yr_ref, ya_ref, ys_ref, gates_ref, wb_ref, wo_ref, g_ref, b_ref, o_ref):
    merged = None
    for i, y_ref in enumerate((yr_ref, ya_ref, ys_ref)):
        gate = gates_ref[:, i * D_MODEL:(i + 1) * D_MODEL].astype(F32)
        term = gate * jnp.dot(y_ref[...], wb_ref[i], preferred_element_type=F32)
        merged = term if merged is None else merged + term
    proj = jnp.dot(merged.astype(BF16), wo_ref[...], preferred_element_type=F32)
    o_ref[...] = _layer_norm_rows(DEEPNORM_ALPHA * x_ref[...] + proj, g_ref[...], b_ref[...])


def _merge(x2d, y_ret, y_att, y_sgu, gates, wb_bf, wo_bf, ln_g, ln_b, *, tb=512):
    t = x2d.shape[0]
    row = lambda i: (i, 0)
    const = lambda i: (0, 0)
    return pl.pallas_call(
        _merge_kernel,
        grid=(t // tb,),
        in_specs=[
            pl.BlockSpec((tb, D_MODEL), row),
            pl.BlockSpec((tb, D_MODEL), row),
            pl.BlockSpec((tb, D_MODEL), row),
            pl.BlockSpec((tb, D_MODEL), row),
            pl.BlockSpec((tb, GATE_W), row),
            pl.BlockSpec((N_BRANCH, D_MODEL, D_MODEL), lambda i: (0, 0, 0)),
            pl.BlockSpec((D_MODEL, D_MODEL), const),
            pl.BlockSpec((1, D_MODEL), const),
            pl.BlockSpec((1, D_MODEL), const),
        ],
        out_specs=pl.BlockSpec((tb, D_MODEL), row),
        out_shape=jax.ShapeDtypeStruct((t, D_MODEL), F32),
        compiler_params=pltpu.CompilerParams(dimension_semantics=("arbitrary",),
                                             vmem_limit_bytes=VMEM_LIMIT_BYTES),
        name="merge",
    )(x2d, y_ret, y_att, y_sgu, gates, wb_bf, wo_bf, ln_g, ln_b)


PEER_PAD_ID = 1024.0


def _peer_candidate_ids():
    pairs = [(0, b) for b in range(PEER_TOPK)]
    pairs += [(a, b) for a in range(1, SUBLANES) for b in range(SUBLANES)]
    pairs += [(a, 0) for a in range(SUBLANES, PEER_TOPK)]
    ids = [float(a * PEER_TOPK + b) if (a + 1) * (b + 1) <= PEER_TOPK else PEER_PAD_ID for a, b in pairs]
    return np.asarray(ids, np.float32)


def _topk_rows(v, ids, k):
    vals, idxs = [], []
    for _ in range(k):
        m = jnp.max(v, axis=0, keepdims=True)
        idx = jnp.min(jnp.where(v == m, ids, 2.0 * PEER_PAD_ID), axis=0, keepdims=True)
        vals.append(m)
        idxs.append(idx)
        v = jnp.where(ids == idx, NEG_INF, v)
    return vals, idxs


def _route_kernel(x_ref, wq_ref, keys_ref, cid_ref, i1_ref, i2_ref, gate_ref):
    tb = x_ref.shape[0]
    q = jnp.dot(x_ref[...].astype(BF16), wq_ref[...], preferred_element_type=F32).astype(BF16)
    half = PEER_DQ // 2
    key_row = lax.broadcasted_iota(jnp.int32, (PEER_NKEYS, tb), 0).astype(F32)
    arow = lax.broadcasted_iota(jnp.int32, (PEER_TOPK, tb), 0).astype(F32)
    cand_ids = cid_ref[...]
    i1_rows, i2_rows, gate_rows = [], [], []
    for h in range(PEER_HEADS):
        sub = []
        for p in range(2):
            qs = q[:, (h * 2 + p) * half:(h * 2 + p + 1) * half]
            sc = lax.dot_general(keys_ref[h * 2 + p], qs, (((1,), (1,)), ((), ())),
                                 preferred_element_type=F32)
            sub.append(_topk_rows(sc, key_row, PEER_TOPK))
        (s1, i1), (s2, i2) = sub
        s1m = jnp.concatenate(s1, axis=0)
        s2m = jnp.concatenate(s2, axis=0)
        i1m = jnp.concatenate(i1, axis=0)
        i2m = jnp.concatenate(i2, axis=0)
        pieces = [s1[0] + s2m[:SUBLANES], s1[0] + s2m[SUBLANES:]]
        pieces += [s1[a] + s2m[:SUBLANES] for a in range(1, SUBLANES)]
        pieces += [s1m[SUBLANES:] + s2[0]]
        cand = jnp.where(cand_ids < PEER_PAD_ID, jnp.concatenate(pieces, axis=0), NEG_INF)
        top_s, top_pos = _topk_rows(cand, cand_ids, PEER_TOPK)
        top_s = jnp.concatenate(top_s, axis=0)
        e = jnp.exp(top_s - top_s[0:1, :])
        gate_rows.append(e / jnp.sum(e, axis=0, keepdims=True))
        for pos in top_pos:
            a = jnp.floor(pos * (1.0 / PEER_TOPK))
            b = pos - a * PEER_TOPK
            i1_rows.append(jnp.sum(jnp.where(arow == a, i1m, 0.0), axis=0, keepdims=True))
            i2_rows.append(jnp.sum(jnp.where(arow == b, i2m, 0.0), axis=0, keepdims=True))
    i1_ref[...] = jnp.concatenate(i1_rows, axis=0).T
    i2_ref[...] = jnp.concatenate(i2_rows, axis=0).T
    gate_ref[...] = jnp.concatenate(gate_rows, axis=0).T


def _route(x1, wq_bf, keys_bf, *, tb=128):
    t = x1.shape[0]
    hk = PEER_HEADS * PEER_TOPK
    row = lambda i: (i, 0)
    ids = _peer_candidate_ids()
    cand_ids = jnp.asarray(np.ascontiguousarray(np.broadcast_to(ids[:, None], (ids.shape[0], tb))))
    return pl.pallas_call(
        _route_kernel,
        grid=(t // tb,),
        in_specs=[
            pl.BlockSpec((tb, D_MODEL), row),
            pl.BlockSpec((D_MODEL, PEER_HEADS * PEER_DQ), lambda i: (0, 0)),
            pl.BlockSpec((PEER_HEADS * 2, PEER_NKEYS, PEER_DQ // 2), lambda i: (0, 0, 0)),
            pl.BlockSpec(cand_ids.shape, lambda i: (0, 0)),
        ],
        out_specs=[pl.BlockSpec((tb, hk), row)] * 3,
        out_shape=[jax.ShapeDtypeStruct((t, hk), F32)] * 3,
        compiler_params=pltpu.CompilerParams(dimension_semantics=("arbitrary",),
                                             vmem_limit_bytes=VMEM_LIMIT_BYTES),
        name="peer_route",
    )(x1, wq_bf, keys_bf, cand_ids)


def _peer_kernel(x_ref, i1_ref, i2_ref, gate_ref, ut_ref, v_ref, g_ref, b_ref, o_ref,
                 gs_ref, xb_ref, acc_ref, *, tb, ec, pitch, unroll):
    e = pl.program_id(1)
    ncl = ec // PEER_NKEYS

    @pl.when(e == 0)
    def _():
        xb_ref[...] = x_ref[...].astype(BF16)
        acc_ref[...] = jnp.zeros_like(acc_ref)
        key_id = lax.broadcasted_iota(jnp.int32, (PEER_NKEYS, PEER_NKEYS), 0).astype(F32)

        def build(blk, carry):
            for r in range(unroll):
                t = blk * unroll + r
                i1 = jnp.broadcast_to(i1_ref[pl.ds(t, 1), :], (PEER_NKEYS, PEER_NKEYS))
                i2 = jnp.broadcast_to(i2_ref[pl.ds(t, 1), :], (PEER_NKEYS, PEER_NKEYS))
                gt = jnp.broadcast_to(gate_ref[pl.ds(t, 1), :], (PEER_NKEYS, PEER_NKEYS))
                a_t = jnp.where(i1 == key_id, gt, 0.0).astype(BF16)
                b_t = jnp.where(i2 == key_id, 1.0, 0.0).astype(BF16)
                g_tok = lax.dot_general(a_t, b_t, (((1,), (1,)), ((), ())), preferred_element_type=F32)
                g_rnd = g_tok.astype(BF16).astype(F32)
                hi = lax.bitcast_convert_type(g_rnd[:PEER_PLANES], jnp.uint32)
                lo = lax.bitcast_convert_type(g_rnd[PEER_PLANES:], jnp.uint32) >> 16
                gs_ref[pl.ds(t, PEER_PLANES, stride=pitch), :] = hi | lo
            return carry

        lax.fori_loop(0, tb // unroll, build, 0)

    h = jnp.dot(xb_ref[...], ut_ref[...], preferred_element_type=F32)
    act = _gelu(h)
    parts = []
    for cl in range(ncl):
        c = e * ncl + cl
        upper = c >= PEER_PLANES
        plane = jnp.where(upper, c - PEER_PLANES, c)
        shift = jnp.where(upper, 16, 0).astype(jnp.uint32)
        word = gs_ref[pl.ds(pl.multiple_of(plane * pitch, SUBLANES), tb), :]
        gval = lax.bitcast_convert_type((word << shift) & jnp.uint32(0xFFFF0000), F32)
        parts.append(act[:, cl * PEER_NKEYS:(cl + 1) * PEER_NKEYS] * gval)
    w = jnp.concatenate(parts, axis=1).astype(BF16)
    acc_ref[...] += jnp.dot(w, v_ref[...], preferred_element_type=F32)

    @pl.when(e == pl.num_programs(1) - 1)
    def _():
        o_ref[...] = _layer_norm_rows(DEEPNORM_ALPHA * x_ref[...] + acc_ref[...], g_ref[...], b_ref[...])


def _peer(x1, i1, i2, gate, ut_bf, v_bf, ln_g, ln_b, *, tb=512, ec=1024, unroll=32):
    t = x1.shape[0]
    n_exp = v_bf.shape[0]
    hk = PEER_HEADS * PEER_TOPK
    pitch = tb + SUBLANES
    row = lambda i, e: (i, 0)
    const = lambda i, e: (0, 0)
    return pl.pallas_call(
        functools.partial(_peer_kernel, tb=tb, ec=ec, pitch=pitch, unroll=unroll),
        grid=(t // tb, n_exp // ec),
        in_specs=[
            pl.BlockSpec((tb, D_MODEL), row),
            pl.BlockSpec((tb, hk), row),
            pl.BlockSpec((tb, hk), row),
            pl.BlockSpec((tb, hk), row),
            pl.BlockSpec((D_MODEL, ec), lambda i, e: (0, e)),
            pl.BlockSpec((ec, D_MODEL), lambda i, e: (e, 0)),
            pl.BlockSpec((1, D_MODEL), const),
            pl.BlockSpec((1, D_MODEL), const),
        ],
        out_specs=pl.BlockSpec((tb, D_MODEL), row),
        out_shape=jax.ShapeDtypeStruct((t, D_MODEL), F32),
        scratch_shapes=[
            pltpu.VMEM((PEER_PLANES * pitch, PEER_NKEYS), jnp.uint32),
            pltpu.VMEM((tb, D_MODEL), BF16),
            pltpu.VMEM((tb, D_MODEL), F32),
        ],
        compiler_params=pltpu.CompilerParams(dimension_semantics=("arbitrary", "arbitrary"),
                                             vmem_limit_bytes=VMEM_LIMIT_BYTES),
        name="peer_dense",
    )(x1, i1, i2, gate, ut_bf, v_bf, ln_g, ln_b)


def _rope_tables(seq):
    rows = seq // GRID_W
    row = jnp.repeat(jnp.arange(rows, dtype=F32), GRID_W)
    col = jnp.tile(jnp.arange(GRID_W, dtype=F32), rows)
    n_freq = ATT_DH // 4
    inv_freq = ROPE_THETA ** (-jnp.arange(n_freq, dtype=F32) / n_freq)
    ang = jnp.concatenate([row[:, None] * inv_freq, col[:, None] * inv_freq], axis=-1)
    cos, sin = jnp.cos(ang), jnp.sin(ang)
    return jnp.concatenate([cos, cos], axis=-1), jnp.concatenate([-sin, sin], axis=-1)


def _layer(x2d, batch, seq, cos2, sin2, p):
    rq, rk, rv, rg, aq, ak, av, y_sgu, gates = _inproj(
        x2d, seq, p["w_in"], cos2, sin2, p["b_gate"], p["qn"], p["kn"], p["sgu_ln_g"], p["sgu_ln_b"],
        p["sgu_w"], p["sgu_bt"])
    y_ret = _retention(rq, rk, rv, rg, p["lg"], batch, seq)
    y_att = _attention(aq, ak, av, batch, seq)
    x1 = _merge(x2d, y_ret, y_att, y_sgu, gates, p["w_branch"], p["w_out"], p["ln1_g"], p["ln1_b"])
    i1, i2, gate = _route(x1, p["peer_wq"], p["peer_keys"])
    return _peer(x1, i1, i2, gate, p["peer_ut"], p["peer_v"], p["ln2_g"], p["ln2_b"])


def _prep_layer(l, w_in, b_gate, ret_decay_fwd, ret_decay_bwd, attn_q_norm, attn_k_norm,
                sgu_ln_g, sgu_ln_b, sgu_w, sgu_b, w_branch, w_out, ln1_g, ln1_b,
                peer_wq, peer_keys, peer_u, peer_v, ln2_g, ln2_b):
    half = PEER_DQ // 2
    return dict(
        w_in=w_in[l].astype(BF16),
        b_gate=b_gate[l].reshape(1, GATE_W),
        lg=jnp.stack([jax.nn.log_sigmoid(ret_decay_fwd[l].astype(F32)),
                      jax.nn.log_sigmoid(ret_decay_bwd[l].astype(F32))]),
        qn=attn_q_norm[l].reshape(1, ATT_DH),
        kn=attn_k_norm[l].reshape(1, ATT_DH),
        sgu_ln_g=sgu_ln_g[l].reshape(1, SGU_WIDTH),
        sgu_ln_b=sgu_ln_b[l].reshape(1, SGU_WIDTH),
        sgu_w=sgu_w[l].astype(BF16),
        sgu_bt=jnp.transpose(sgu_b[l]),
        w_branch=w_branch[l].astype(BF16),
        w_out=w_out[l].astype(BF16),
        ln1_g=ln1_g[l].reshape(1, D_MODEL),
        ln1_b=ln1_b[l].reshape(1, D_MODEL),
        peer_wq=peer_wq[l].astype(BF16),
        peer_keys=peer_keys[l].reshape(PEER_HEADS * 2, PEER_NKEYS, half).astype(BF16),
        peer_ut=jnp.transpose(peer_u[l].astype(BF16)),
        peer_v=peer_v[l].astype(BF16),
        ln2_g=ln2_g[l].reshape(1, D_MODEL),
        ln2_b=ln2_b[l].reshape(1, D_MODEL),
    )


def kernel(x_prompt, x_sample, w_in, b_gate, ret_decay_fwd, ret_decay_bwd, attn_q_norm, attn_k_norm,
           sgu_ln_g, sgu_ln_b, sgu_w, sgu_b, w_branch, w_out, ln1_g, ln1_b,
           peer_wq, peer_keys, peer_u, peer_v, ln2_g, ln2_b):
    weights = (w_in, b_gate, ret_decay_fwd, ret_decay_bwd, attn_q_norm, attn_k_norm,
               sgu_ln_g, sgu_ln_b, sgu_w, sgu_b, w_branch, w_out, ln1_g, ln1_b,
               peer_wq, peer_keys, peer_u, peer_v, ln2_g, ln2_b)
    layers = [_prep_layer(l, *weights) for l in range(DEPTH)]
    outs = []
    for x in (x_prompt, x_sample):
        batch, seq, _ = x.shape
        cos2, sin2 = _rope_tables(seq)
        h = x.reshape(batch * seq, D_MODEL)
        for p in layers:
            h = _layer(h, batch, seq, cos2, sin2, p)
        outs.append(h.reshape(batch, seq, D_MODEL))
    return tuple(outs)
```

```python
import functools
import math

import jax
import jax.numpy as jnp
import numpy as np
from jax import lax
from jax.experimental import pallas as pl
from jax.experimental.pallas import tpu as pltpu

F32 = jnp.float32
BF16 = jnp.bfloat16
FP8 = jnp.float8_e4m3fn

D_MODEL = 1024
DEPTH = 2
GRID_W = 64
CHUNK = 128
RET_HEADS = 4
RET_DK = 128
RET_DV = 256
ATT_HEADS = 8
ATT_KV_HEADS = 2
ATT_DH = 128
ATT_GROUP = ATT_HEADS // ATT_KV_HEADS
SGU_GROUPS = 4
SGU_WIDTH = 1024
SGU_GW = SGU_WIDTH // SGU_GROUPS
PEER_HEADS = 8
PEER_NKEYS = 128
PEER_DQ = 256
PEER_TOPK = 16
PEER_PLANES = PEER_NKEYS // 2
N_BRANCH = 3
ROPE_THETA = 10000.0
DEEPNORM_ALPHA = (2 * DEPTH) ** 0.25
LN_EPS = 1e-5
RMS_EPS = 1e-6

RET_QK_W = RET_HEADS * RET_DK
RET_V_W = RET_HEADS * RET_DV
ATT_Q_W = ATT_HEADS * ATT_DH
ATT_KV_W = ATT_KV_HEADS * ATT_DH
GATE_W = N_BRANCH * D_MODEL
SPLITS = (RET_QK_W, RET_QK_W, RET_V_W, RET_V_W, ATT_Q_W, ATT_KV_W, ATT_KV_W, SGU_WIDTH, SGU_WIDTH, GATE_W)
OFFS = tuple(int(sum(SPLITS[:i])) for i in range(len(SPLITS) + 1))
IN_W = OFFS[-1]

LANES = 128
SUBLANES = 8
VMEM_LIMIT_BYTES = 56 * 1024 * 1024

NEG_INF = float("-inf")
FP8_TARGET = 224.0
FP8_TINY = 1e-30
ATT_Q_SCALE = ATT_DH ** -0.5 * math.log2(math.e)


def _gelu(x):
    return 0.5 * x * (1.0 + lax.erf(x * (2.0 ** -0.5)))


def _rope(x, cos2, sin2):
    return x * cos2 + pltpu.roll(x, ATT_DH // 2, 1) * sin2


def _layer_norm_rows(x, g, b):
    mu = jnp.mean(x, axis=-1, keepdims=True)
    xc = x - mu
    var = jnp.mean(xc * xc, axis=-1, keepdims=True)
    return xc * lax.rsqrt(var + LN_EPS) * g + b


def _inproj_kernel(x_ref, w_ref, cos_ref, sin_ref, bgate_ref, qn_ref, kn_ref, lng_ref, lnb_ref,
                   sw_ref, sb_ref,
                   rq_ref, rk_ref, rv_ref, rg_ref, aq_ref, ak_ref, av_ref, ysgu_ref, gates_ref):
    tb = x_ref.shape[0]
    xb = x_ref[...].astype(BF16)
    cos2 = cos_ref[...]
    sin2 = sin_ref[...]

    def proj(i):
        return jnp.dot(xb, w_ref[:, OFFS[i]:OFFS[i + 1]], preferred_element_type=F32)

    h = proj(0)
    for hd in range(RET_HEADS):
        sl = slice(hd * RET_DK, (hd + 1) * RET_DK)
        rq_ref[:, sl] = _rope(h[:, sl], cos2, sin2).astype(BF16)
    h = proj(1)
    for hd in range(RET_HEADS):
        sl = slice(hd * RET_DK, (hd + 1) * RET_DK)
        rk_ref[:, sl] = (_rope(h[:, sl], cos2, sin2) * (RET_DK ** -0.5)).astype(BF16)
    rv_ref[...] = proj(2).astype(BF16)
    h = proj(3)
    rg_ref[...] = (h * jax.nn.sigmoid(h)).astype(BF16)

    h = proj(4)
    qn = qn_ref[...]
    for hd in range(ATT_HEADS):
        sl = slice(hd * ATT_DH, (hd + 1) * ATT_DH)
        t = h[:, sl]
        t = t * lax.rsqrt(jnp.mean(t * t, axis=-1, keepdims=True) + RMS_EPS) * qn
        aq_ref[:, sl] = (_rope(t, cos2, sin2) * ATT_Q_SCALE).astype(BF16)
    h = proj(5)
    kn = kn_ref[...]
    for hd in range(ATT_KV_HEADS):
        sl = slice(hd * ATT_DH, (hd + 1) * ATT_DH)
        t = h[:, sl]
        t = t * lax.rsqrt(jnp.mean(t * t, axis=-1, keepdims=True) + RMS_EPS) * kn
        ak_ref[:, sl] = _rope(t, cos2, sin2).astype(BF16)
    av_ref[...] = proj(6).astype(BF16)

    u = _gelu(proj(7))
    vn = _layer_norm_rows(_gelu(proj(8)), lng_ref[...], lnb_ref[...]).astype(BF16)
    sb = sb_ref[...]
    for c in range(tb // CHUNK):
        rows = slice(c * CHUNK, (c + 1) * CHUNK)
        for g in range(SGU_GROUPS):
            cols = slice(g * SGU_GW, (g + 1) * SGU_GW)
            mixed = jnp.dot(sw_ref[g], vn[rows, cols], preferred_element_type=F32) + sb[:, g:g + 1]
            ysgu_ref[rows, cols] = (u[rows, cols] * mixed).astype(BF16)

    gates_ref[...] = jax.nn.sigmoid(proj(9) + bgate_ref[...]).astype(BF16)


def _inproj(x2d, seq, w_bf, cos2, sin2, b_gate, qn, kn, lng, lnb, sgu_w_bf, sgu_bt, *, tb=256):
    t = x2d.shape[0]
    nblk_seq = seq // tb
    const = lambda i: (0, 0)
    row = lambda i: (i, 0)
    pos = lambda i: (i % nblk_seq, 0)
    outs = [(RET_QK_W, BF16), (RET_QK_W, BF16), (RET_V_W, BF16), (RET_V_W, BF16), (ATT_Q_W, BF16),
            (ATT_KV_W, BF16), (ATT_KV_W, BF16), (SGU_WIDTH, BF16), (GATE_W, BF16)]
    return pl.pallas_call(
        _inproj_kernel,
        grid=(t // tb,),
        in_specs=[
            pl.BlockSpec((tb, D_MODEL), row),
            pl.BlockSpec((D_MODEL, IN_W), const, pipeline_mode=pl.Buffered(1)),
            pl.BlockSpec((tb, ATT_DH), pos),
            pl.BlockSpec((tb, ATT_DH), pos),
            pl.BlockSpec((1, GATE_W), const),
            pl.BlockSpec((1, ATT_DH), const),
            pl.BlockSpec((1, ATT_DH), const),
            pl.BlockSpec((1, SGU_WIDTH), const),
            pl.BlockSpec((1, SGU_WIDTH), const),
            pl.BlockSpec((SGU_GROUPS, CHUNK, CHUNK), lambda i: (0, 0, 0)),
            pl.BlockSpec((CHUNK, SGU_GROUPS), const),
        ],
        out_specs=[pl.BlockSpec((tb, w), row) for w, _ in outs],
        out_shape=[jax.ShapeDtypeStruct((t, w), d) for w, d in outs],
        compiler_params=pltpu.CompilerParams(dimension_semantics=("arbitrary",),
                                             vmem_limit_bytes=VMEM_LIMIT_BYTES),
        name="inproj",
    )(x2d, w_bf, cos2, sin2, b_gate, qn, kn, lng, lnb, sgu_w_bf, sgu_bt)


def _retention_kernel(lg_ref, q_ref, k_ref, v_ref, *rest, rb, forward):
    if forward:
        g_ref, yb_ref, o_ref, state_ref = rest
    else:
        o_ref, state_ref = rest
    nchunk = rb // CHUNK

    @pl.when(pl.program_id(1) == 0)
    def _():
        state_ref[...] = jnp.zeros_like(state_ref)

    n_i = lax.broadcasted_iota(jnp.int32, (CHUNK, CHUNK), 0)
    m_i = lax.broadcasted_iota(jnp.int32, (CHUNK, CHUNK), 1)
    col = lax.broadcasted_iota(jnp.int32, (CHUNK, 1), 0).astype(F32)
    dist = ((n_i - m_i) if forward else (m_i - n_i)).astype(F32)
    live = (n_i >= m_i) if forward else (m_i > n_i)

    for hd in range(RET_HEADS):
        lg = lg_ref[0 if forward else 1, hd]
        decay = jnp.where(live, jnp.exp(lg * jnp.maximum(dist, 0.0)), 0.0)
        zeta = jnp.exp(lg * ((CHUNK - 1.0 - col) if forward else col))
        xi = jnp.exp(lg * ((col + 1.0) if forward else (CHUNK - col)))
        chunk_decay = jnp.exp(lg * CHUNK)
        qk_cols = slice(hd * RET_DK, (hd + 1) * RET_DK)
        v_cols = slice(hd * RET_DV, (hd + 1) * RET_DV)
        for c in (range(nchunk) if forward else reversed(range(nchunk))):
            rows = slice(c * CHUNK, (c + 1) * CHUNK)
            q = q_ref[rows, qk_cols]
            k = k_ref[rows, qk_cols]
            v = v_ref[rows, v_cols]
            sc = lax.dot_general(q, k, (((1,), (1,)), ((), ())), preferred_element_type=F32) * decay
            y = jnp.dot(sc.astype(BF16), v, preferred_element_type=F32)
            st = state_ref[hd]
            y = y + jnp.dot(q, st.astype(BF16), preferred_element_type=F32) * xi
            vz = (v.astype(F32) * zeta).astype(BF16)
            upd = lax.dot_general(k, vz, (((0,), (0,)), ((), ())), preferred_element_type=F32)
            state_ref[hd] = st * chunk_decay + upd
            if forward:
                y = y + yb_ref[rows, v_cols]
                mu = jnp.mean(y, axis=-1, keepdims=True)
                yc = y - mu
                var = jnp.mean(yc * yc, axis=-1, keepdims=True)
                yn = yc * lax.rsqrt(var + LN_EPS)
                o_ref[rows, v_cols] = (g_ref[rows, v_cols].astype(F32) * yn).astype(BF16)
            else:
                o_ref[rows, v_cols] = y


def _retention_pass(lg, rq, rk, rv, extra, batch, seq, *, rb, forward):
    t = rq.shape[0]
    ns = seq // rb
    if forward:
        blk = lambda b, s, lg_ref: (b * ns + s, 0)
    else:
        blk = lambda b, s, lg_ref: (b * ns + ns - 1 - s, 0)
    grid_spec = pltpu.PrefetchScalarGridSpec(
        num_scalar_prefetch=1,
        grid=(batch, ns),
        in_specs=[pl.BlockSpec((rb, RET_QK_W), blk), pl.BlockSpec((rb, RET_QK_W), blk),
                  pl.BlockSpec((rb, RET_V_W), blk)] + [pl.BlockSpec((rb, RET_V_W), blk)] * len(extra),
        out_specs=pl.BlockSpec((rb, RET_V_W), blk),
        scratch_shapes=[pltpu.VMEM((RET_HEADS, RET_DK, RET_DV), F32)],
    )
    return pl.pallas_call(
        functools.partial(_retention_kernel, rb=rb, forward=forward),
        grid_spec=grid_spec,
        out_shape=jax.ShapeDtypeStruct((t, RET_V_W), BF16 if forward else F32),
        compiler_params=pltpu.CompilerParams(dimension_semantics=("arbitrary", "arbitrary"),
                                             vmem_limit_bytes=VMEM_LIMIT_BYTES),
        name="retention_fwd" if forward else "retention_bwd",
    )(lg, rq, rk, rv, *extra)


def _retention(rq, rk, rv, rg, lg, batch, seq, *, rb=512):
    y_bwd = _retention_pass(lg, rq, rk, rv, (), batch, seq, rb=rb, forward=False)
    return _retention_pass(lg, rq, rk, rv, (rg, y_bwd), batch, seq, rb=rb, forward=True)


def _attention_kernel(q_ref, k_ref, vt_ref, o_ref, m_ref, l_ref, acc_ref):
    ki = pl.program_id(3)

    @pl.when(ki == 0)
    def _():
        m_ref[...] = jnp.full_like(m_ref, NEG_INF)
        l_ref[...] = jnp.zeros_like(l_ref)
        acc_ref[...] = jnp.zeros_like(acc_ref)

    k = k_ref[...]
    vt = vt_ref[...]
    for g in range(ATT_GROUP):
        q = q_ref[:, g * ATT_DH:(g + 1) * ATT_DH]
        st = lax.dot_general(k, q, (((1,), (1,)), ((), ())), preferred_element_type=F32)
        m_prev = m_ref[g:g + 1, :]
        m_new = jnp.maximum(m_prev, jnp.max(st, axis=0, keepdims=True))
        alpha = jnp.exp2(m_prev - m_new)
        pt = jnp.exp2(st - m_new)
        l_ref[g:g + 1, :] = alpha * l_ref[g:g + 1, :] + jnp.sum(pt, axis=0, keepdims=True)
        acc_ref[g] = alpha * acc_ref[g] + jnp.dot(vt, pt.astype(BF16), preferred_element_type=F32)
        m_ref[g:g + 1, :] = m_new

    @pl.when(ki == pl.num_programs(3) - 1)
    def _():
        for g in range(ATT_GROUP):
            out = acc_ref[g] / l_ref[g:g + 1, :]
            o_ref[:, g * ATT_DH:(g + 1) * ATT_DH] = out.T.astype(BF16)


def _attention(aq, ak, avt, batch, seq, *, tq=512, tk=4096):
    t = aq.shape[0]
    nq = seq // tq
    nk = seq // tk
    gw = ATT_GROUP * ATT_DH
    return pl.pallas_call(
        _attention_kernel,
        grid=(batch, ATT_KV_HEADS, nq, nk),
        in_specs=[
            pl.BlockSpec((tq, gw), lambda b, g, qi, ki: (b * nq + qi, g)),
            pl.BlockSpec((tk, ATT_DH), lambda b, g, qi, ki: (b * nk + ki, g)),
            pl.BlockSpec((ATT_DH, tk), lambda b, g, qi, ki: (g, b * nk + ki)),
        ],
        out_specs=pl.BlockSpec((tq, gw), lambda b, g, qi, ki: (b * nq + qi, g)),
        out_shape=jax.ShapeDtypeStruct((t, ATT_Q_W), BF16),
        scratch_shapes=[
            pltpu.VMEM((SUBLANES, tq), F32),
            pltpu.VMEM((SUBLANES, tq), F32),
            pltpu.VMEM((ATT_GROUP, ATT_DH, tq), F32),
        ],
        compiler_params=pltpu.CompilerParams(
            dimension_semantics=("arbitrary", "arbitrary", "arbitrary", "arbitrary"),
            vmem_limit_bytes=VMEM_LIMIT_BYTES),
        name="attention",
    )(aq, ak, avt)


def _merge_kernel(x_ref, yr_ref, ya_ref, ys_ref, gates_ref, wb_ref, wo_ref, g_ref, b_ref, o_ref):
    merged = None
    for i, y_ref in enumerate((yr_ref, ya_ref, ys_ref)):
        gate = gates_ref[:, i * D_MODEL:(i + 1) * D_MODEL].astype(F32)
        term = gate * jnp.dot(y_ref[...], wb_ref[i], preferred_element_type=F32)
        merged = term if merged is None else merged + term
    proj = jnp.dot(merged.astype(BF16), wo_ref[...], preferred_element_type=F32)
    o_ref[...] = _layer_norm_rows(DEEPNORM_ALPHA * x_ref[...] + proj, g_ref[...], b_ref[...])


def _merge(x2d, y_ret, y_att, y_sgu, gates, wb_bf, wo_bf, ln_g, ln_b, *, tb=512):
    t = x2d.shape[0]
    row = lambda i: (i, 0)
    const = lambda i: (0, 0)
    return pl.pallas_call(
        _merge_kernel,
        grid=(t // tb,),
        in_specs=[
            pl.BlockSpec((tb, D_MODEL), row),
            pl.BlockSpec((tb, D_MODEL), row),
            pl.BlockSpec((tb, D_MODEL), row),
            pl.BlockSpec((tb, D_MODEL), row),
            pl.BlockSpec((tb, GATE_W), row),
            pl.BlockSpec((N_BRANCH, D_MODEL, D_MODEL), lambda i: (0, 0, 0)),
            pl.BlockSpec((D_MODEL, D_MODEL), const),
            pl.BlockSpec((1, D_MODEL), const),
            pl.BlockSpec((1, D_MODEL), const),
        ],
        out_specs=pl.BlockSpec((tb, D_MODEL), row),
        out_shape=jax.ShapeDtypeStruct((t, D_MODEL), F32),
        compiler_params=pltpu.CompilerParams(dimension_semantics=("arbitrary",),
                                             vmem_limit_bytes=VMEM_LIMIT_BYTES),
        name="merge",
    )(x2d, y_ret, y_att, y_sgu, gates, wb_bf, wo_bf, ln_g, ln_b)


PEER_PAD_ID = 1024.0


def _peer_candidate_ids():
    pairs = [(0, b) for b in range(PEER_TOPK)]
    pairs += [(a, b) for a in range(1, SUBLANES) for b in range(SUBLANES)]
    pairs += [(a, 0) for a in range(SUBLANES, PEER_TOPK)]
    ids = [float(a * PEER_TOPK + b) if (a + 1) * (b + 1) <= PEER_TOPK else PEER_PAD_ID for a, b in pairs]
    return np.asarray(ids, np.float32)


def _topk_rows(v, ids, k):
    vals, idxs = [], []
    for _ in range(k):
        m = jnp.max(v, axis=0, keepdims=True)
        idx = jnp.min(jnp.where(v == m, ids, 2.0 * PEER_PAD_ID), axis=0, keepdims=True)
        vals.append(m)
        idxs.append(idx)
        v = jnp.where(ids == idx, NEG_INF, v)
    return vals, idxs


def _route_kernel(x_ref, wq_ref, keys_ref, cid_ref, i1_ref, i2_ref, gate_ref):
    tb = x_ref.shape[0]
    q = jnp.dot(x_ref[...].astype(BF16), wq_ref[...], preferred_element_type=F32).astype(BF16)
    half = PEER_DQ // 2
    key_row = lax.broadcasted_iota(jnp.int32, (PEER_NKEYS, tb), 0).astype(F32)
    arow = lax.broadcasted_iota(jnp.int32, (PEER_TOPK, tb), 0).astype(F32)
    cand_ids = cid_ref[...]
    i1_rows, i2_rows, gate_rows = [], [], []
    for h in range(PEER_HEADS):
        sub = []
        for p in range(2):
            qs = q[:, (h * 2 + p) * half:(h * 2 + p + 1) * half]
            sc = lax.dot_general(keys_ref[h * 2 + p], qs, (((1,), (1,)), ((), ())),
                                 preferred_element_type=F32)
            sub.append(_topk_rows(sc, key_row, PEER_TOPK))
        (s1, i1), (s2, i2) = sub
        s1m = jnp.concatenate(s1, axis=0)
        s2m = jnp.concatenate(s2, axis=0)
        i1m = jnp.concatenate(i1, axis=0)
        i2m = jnp.concatenate(i2, axis=0)
        pieces = [s1[0] + s2m[:SUBLANES], s1[0] + s2m[SUBLANES:]]
        pieces += [s1[a] + s2m[:SUBLANES] for a in range(1, SUBLANES)]
        pieces += [s1m[SUBLANES:] + s2[0]]
        cand = jnp.where(cand_ids < PEER_PAD_ID, jnp.concatenate(pieces, axis=0), NEG_INF)
        top_s, top_pos = _topk_rows(cand, cand_ids, PEER_TOPK)
        top_s = jnp.concatenate(top_s, axis=0)
        e = jnp.exp(top_s - top_s[0:1, :])
        gate_rows.append(e / jnp.sum(e, axis=0, keepdims=True))
        for pos in top_pos:
            a = jnp.floor(pos * (1.0 / PEER_TOPK))
            b = pos - a * PEER_TOPK
            i1_rows.append(jnp.sum(jnp.where(arow == a, i1m, 0.0), axis=0, keepdims=True))
            i2_rows.append(jnp.sum(jnp.where(arow == b, i2m, 0.0), axis=0, keepdims=True))
    i1_ref[...] = jnp.concatenate(i1_rows, axis=0).T
    i2_ref[...] = jnp.concatenate(i2_rows, axis=0).T
    gate_ref[...] = jnp.concatenate(gate_rows, axis=0).T


def _route(x1, wq_bf, keys_bf, *, tb=128):
    t = x1.shape[0]
    hk = PEER_HEADS * PEER_TOPK
    row = lambda i: (i, 0)
    ids = _peer_candidate_ids()
    cand_ids = jnp.asarray(np.ascontiguousarray(np.broadcast_to(ids[:, None], (ids.shape[0], tb))))
    return pl.pallas_call(
        _route_kernel,
        grid=(t // tb,),
        in_specs=[
            pl.BlockSpec((tb, D_MODEL), row),
            pl.BlockSpec((D_MODEL, PEER_HEADS * PEER_DQ), lambda i: (0, 0)),
            pl.BlockSpec((PEER_HEADS * 2, PEER_NKEYS, PEER_DQ // 2), lambda i: (0, 0, 0)),
            pl.BlockSpec(cand_ids.shape, lambda i: (0, 0)),
        ],
        out_specs=[pl.BlockSpec((tb, hk), row)] * 3,
        out_shape=[jax.ShapeDtypeStruct((t, hk), F32)] * 3,
        compiler_params=pltpu.CompilerParams(dimension_semantics=("arbitrary",),
                                             vmem_limit_bytes=VMEM_LIMIT_BYTES),
        name="peer_route",
    )(x1, wq_bf, keys_bf, cand_ids)


def _fp8_scale(bound):
    return jnp.exp2(jnp.floor(jnp.log2(FP8_TARGET / jnp.maximum(bound, FP8_TINY))))


def _peer_kernel(x_ref, i1_ref, i2_ref, gate_ref, ut_ref, stat_ref, v_ref, g_ref, b_ref, o_ref,
                 gs_ref, x8_ref, hscale_ref, wscale_ref, acc_ref, *, tb, ec, pitch, unroll):
    e = pl.program_id(1)
    ncl = ec // PEER_NKEYS

    @pl.when(e == 0)
    def _():
        x = x_ref[...]
        sx = _fp8_scale(jnp.max(jnp.max(jnp.abs(x), axis=0, keepdims=True), axis=1, keepdims=True))
        x8_ref[...] = (x * sx).astype(FP8)
        hscale_ref[...] = jnp.broadcast_to(stat_ref[0:1, 0:1] / sx, hscale_ref.shape)
        row_norm = jnp.sqrt(jnp.sum(x * x, axis=1, keepdims=True))
        sw = _fp8_scale(PEER_HEADS * row_norm * stat_ref[0:1, 1:2])
        wscale_ref[...] = jnp.broadcast_to(sw, wscale_ref.shape)
        acc_ref[...] = jnp.zeros_like(acc_ref)
        key_id = lax.broadcasted_iota(jnp.int32, (PEER_NKEYS, PEER_NKEYS), 0).astype(F32)

        def build(blk, carry):
            for r in range(unroll):
                t = blk * unroll + r
                i1 = jnp.broadcast_to(i1_ref[pl.ds(t, 1), :], (PEER_NKEYS, PEER_NKEYS))
                i2 = jnp.broadcast_to(i2_ref[pl.ds(t, 1), :], (PEER_NKEYS, PEER_NKEYS))
                gt = jnp.broadcast_to(0.5 * gate_ref[pl.ds(t, 1), :] * wscale_ref[pl.ds(t, 1), :],
                                      (PEER_NKEYS, PEER_NKEYS))
                a_t = jnp.where(i1 == key_id, gt, 0.0).astype(BF16)
                b_t = jnp.where(i2 == key_id, 1.0, 0.0).astype(BF16)
                g_tok = lax.dot_general(a_t, b_t, (((1,), (1,)), ((), ())), preferred_element_type=F32)
                g_rnd = g_tok.astype(BF16).astype(F32)
                hi = lax.bitcast_convert_type(g_rnd[:PEER_PLANES], jnp.uint32)
                lo = lax.bitcast_convert_type(g_rnd[PEER_PLANES:], jnp.uint32) >> 16
                gs_ref[pl.ds(t, PEER_PLANES, stride=pitch), :] = hi | lo
            return carry

        lax.fori_loop(0, tb // unroll, build, 0)

    h = jnp.dot(x8_ref[...], ut_ref[...], preferred_element_type=F32) * hscale_ref[0:1, 0:1]
    act = h * (1.0 + lax.erf(h * (2.0 ** -0.5)))
    parts = []
    for cl in range(ncl):
        c = e * ncl + cl
        upper = c >= PEER_PLANES
        plane = jnp.where(upper, c - PEER_PLANES, c)
        shift = jnp.where(upper, 16, 0).astype(jnp.uint32)
        word = gs_ref[pl.ds(pl.multiple_of(plane * pitch, SUBLANES), tb), :]
        gval = lax.bitcast_convert_type((word << shift) & jnp.uint32(0xFFFF0000), F32)
        parts.append(act[:, cl * PEER_NKEYS:(cl + 1) * PEER_NKEYS] * gval)
    w = jnp.concatenate(parts, axis=1).astype(FP8)
    acc_ref[...] += jnp.dot(w, v_ref[...], preferred_element_type=F32)

    @pl.when(e == pl.num_programs(1) - 1)
    def _():
        peer = acc_ref[...] * (stat_ref[0:1, 2:3] / wscale_ref[:, 0:1])
        o_ref[...] = _layer_norm_rows(DEEPNORM_ALPHA * x_ref[...] + peer, g_ref[...], b_ref[...])


def _peer(x1, i1, i2, gate, ut_f8, stats, v_f8, ln_g, ln_b, *, tb=512, ec=1024, unroll=32):
    t = x1.shape[0]
    n_exp = v_f8.shape[0]
    hk = PEER_HEADS * PEER_TOPK
    pitch = tb + SUBLANES
    row = lambda i, e: (i, 0)
    const = lambda i, e: (0, 0)
    return pl.pallas_call(
        functools.partial(_peer_kernel, tb=tb, ec=ec, pitch=pitch, unroll=unroll),
        grid=(t // tb, n_exp // ec),
        in_specs=[
            pl.BlockSpec((tb, D_MODEL), row),
            pl.BlockSpec((tb, hk), row),
            pl.BlockSpec((tb, hk), row),
            pl.BlockSpec((tb, hk), row),
            pl.BlockSpec((D_MODEL, ec), lambda i, e: (0, e)),
            pl.BlockSpec((1, LANES), const),
            pl.BlockSpec((ec, D_MODEL), lambda i, e: (e, 0)),
            pl.BlockSpec((1, D_MODEL), const),
            pl.BlockSpec((1, D_MODEL), const),
        ],
        out_specs=pl.BlockSpec((tb, D_MODEL), row),
        out_shape=jax.ShapeDtypeStruct((t, D_MODEL), F32),
        scratch_shapes=[
            pltpu.VMEM((PEER_PLANES * pitch, PEER_NKEYS), jnp.uint32),
            pltpu.VMEM((tb, D_MODEL), FP8),
            pltpu.VMEM((SUBLANES, LANES), F32),
            pltpu.VMEM((tb, LANES), F32),
            pltpu.VMEM((tb, D_MODEL), F32),
        ],
        compiler_params=pltpu.CompilerParams(dimension_semantics=("arbitrary", "arbitrary"),
                                             vmem_limit_bytes=VMEM_LIMIT_BYTES),
        name="peer_dense",
    )(x1, i1, i2, gate, ut_f8, stats, v_f8, ln_g, ln_b)


def _rope_tables(seq):
    rows = seq // GRID_W
    row = jnp.repeat(jnp.arange(rows, dtype=F32), GRID_W)
    col = jnp.tile(jnp.arange(GRID_W, dtype=F32), rows)
    n_freq = ATT_DH // 4
    inv_freq = ROPE_THETA ** (-jnp.arange(n_freq, dtype=F32) / n_freq)
    ang = jnp.concatenate([row[:, None] * inv_freq, col[:, None] * inv_freq], axis=-1)
    cos, sin = jnp.cos(ang), jnp.sin(ang)
    return jnp.concatenate([cos, cos], axis=-1), jnp.concatenate([-sin, sin], axis=-1)


def _layer(x2d, batch, seq, cos2, sin2, p):
    rq, rk, rv, rg, aq, ak, av, y_sgu, gates = _inproj(
        x2d, seq, p["w_in"], cos2, sin2, p["b_gate"], p["qn"], p["kn"], p["sgu_ln_g"], p["sgu_ln_b"],
        p["sgu_w"], p["sgu_bt"])
    y_ret = _retention(rq, rk, rv, rg, p["lg"], batch, seq)
    y_att = _attention(aq, ak, jnp.transpose(av), batch, seq)
    x1 = _merge(x2d, y_ret, y_att, y_sgu, gates, p["w_branch"], p["w_out"], p["ln1_g"], p["ln1_b"])
    i1, i2, gate = _route(x1, p["peer_wq"], p["peer_keys"])
    return _peer(x1, i1, i2, gate, p["peer_ut"], p["peer_stats"], p["peer_v"], p["ln2_g"], p["ln2_b"])


def _prep_layer(l, w_in, b_gate, ret_decay_fwd, ret_decay_bwd, attn_q_norm, attn_k_norm,
                sgu_ln_g, sgu_ln_b, sgu_w, sgu_b, w_branch, w_out, ln1_g, ln1_b,
                peer_wq, peer_keys, peer_u, peer_v, ln2_g, ln2_b):
    half = PEER_DQ // 2
    u_scale = _fp8_scale(jnp.max(jnp.abs(peer_u[l])))
    v_scale = _fp8_scale(jnp.max(jnp.abs(peer_v[l])))
    u_norm = jnp.sqrt(jnp.max(jnp.sum(jnp.square(peer_u[l]), axis=1)))
    stats = jnp.zeros((1, LANES), F32).at[0, 0].set(1.0 / u_scale).at[0, 1].set(u_norm).at[0, 2].set(1.0 / v_scale)
    return dict(
        w_in=w_in[l].astype(BF16),
        b_gate=b_gate[l].reshape(1, GATE_W),
        lg=jnp.stack([jax.nn.log_sigmoid(ret_decay_fwd[l].astype(F32)),
                      jax.nn.log_sigmoid(ret_decay_bwd[l].astype(F32))]),
        qn=attn_q_norm[l].reshape(1, ATT_DH),
        kn=attn_k_norm[l].reshape(1, ATT_DH),
        sgu_ln_g=sgu_ln_g[l].reshape(1, SGU_WIDTH),
        sgu_ln_b=sgu_ln_b[l].reshape(1, SGU_WIDTH),
        sgu_w=sgu_w[l].astype(BF16),
        sgu_bt=jnp.transpose(sgu_b[l]),
        w_branch=w_branch[l].astype(BF16),
        w_out=w_out[l].astype(BF16),
        ln1_g=ln1_g[l].reshape(1, D_MODEL),
        ln1_b=ln1_b[l].reshape(1, D_MODEL),
        peer_wq=peer_wq[l].astype(BF16),
        peer_keys=peer_keys[l].reshape(PEER_HEADS * 2, PEER_NKEYS, half).astype(BF16),
        peer_ut=jnp.transpose(peer_u[l] * u_scale).astype(FP8),
        peer_stats=stats,
        peer_v=(peer_v[l] * v_scale).astype(FP8),
        ln2_g=ln2_g[l].reshape(1, D_MODEL),
        ln2_b=ln2_b[l].reshape(1, D_MODEL),
    )


def kernel(x_prompt, x_sample, w_in, b_gate, ret_decay_fwd, ret_decay_bwd, attn_q_norm, attn_k_norm,
           sgu_ln_g, sgu_ln_b, sgu_w, sgu_b, w_branch, w_out, ln1_g, ln1_b,
           peer_wq, peer_keys, peer_u, peer_v, ln2_g, ln2_b):
    weights = (w_in, b_gate, ret_decay_fwd, ret_decay_bwd, attn_q_norm, attn_k_norm,
               sgu_ln_g, sgu_ln_b, sgu_w, sgu_b, w_branch, w_out, ln1_g, ln1_b,
               peer_wq, peer_keys, peer_u, peer_v, ln2_g, ln2_b)
    layers = [_prep_layer(l, *weights) for l in range(DEPTH)]
    outs = []
    for x in (x_prompt, x_sample):
        batch, seq, _ = x.shape
        cos2, sin2 = _rope_tables(seq)
        h = x.reshape(batch * seq, D_MODEL)
        for p in layers:
            h = _layer(h, batch, seq, cos2, sin2, p)
        outs.append(h.reshape(batch, seq, D_MODEL))
    return tuple(outs)
```

```python
import functools
import math

import jax
import jax.numpy as jnp
import numpy as np
from jax import lax
from jax.experimental import pallas as pl
from jax.experimental.pallas import tpu as pltpu

F32 = jnp.float32
BF16 = jnp.bfloat16
FP8 = jnp.float8_e4m3fn

D_MODEL = 1024
DEPTH = 2
GRID_W = 64
CHUNK = 128
RET_HEADS = 4
RET_DK = 128
RET_DV = 256
ATT_HEADS = 8
ATT_KV_HEADS = 2
ATT_DH = 128
ATT_GROUP = ATT_HEADS // ATT_KV_HEADS
SGU_GROUPS = 4
SGU_WIDTH = 1024
SGU_GW = SGU_WIDTH // SGU_GROUPS
PEER_HEADS = 8
PEER_NKEYS = 128
PEER_DQ = 256
PEER_TOPK = 16
PEER_PLANES = PEER_NKEYS // 2
PEER_PAIR = 16
N_BRANCH = 3
ROPE_THETA = 10000.0
DEEPNORM_ALPHA = (2 * DEPTH) ** 0.25
LN_EPS = 1e-5
RMS_EPS = 1e-6

RET_QK_W = RET_HEADS * RET_DK
RET_V_W = RET_HEADS * RET_DV
ATT_Q_W = ATT_HEADS * ATT_DH
ATT_KV_W = ATT_KV_HEADS * ATT_DH
GATE_W = N_BRANCH * D_MODEL
SPLITS = (RET_QK_W, RET_QK_W, RET_V_W, RET_V_W, ATT_Q_W, ATT_KV_W, ATT_KV_W, SGU_WIDTH, SGU_WIDTH, GATE_W)
OFFS = tuple(int(sum(SPLITS[:i])) for i in range(len(SPLITS) + 1))
IN_W = OFFS[-1]

LANES = 128
SUBLANES = 8
VMEM_LIMIT_BYTES = 56 * 1024 * 1024

NEG_INF = float("-inf")
FP8_TARGET = 224.0
FP8_TINY = 1e-30
ATT_Q_SCALE = ATT_DH ** -0.5 * math.log2(math.e)


def _gelu(x):
    return 0.5 * x * (1.0 + lax.erf(x * (2.0 ** -0.5)))


def _rope(x, cos2, sin2):
    return x * cos2 + pltpu.roll(x, ATT_DH // 2, 1) * sin2


def _layer_norm_rows(x, g, b):
    mu = jnp.mean(x, axis=-1, keepdims=True)
    xc = x - mu
    var = jnp.mean(xc * xc, axis=-1, keepdims=True)
    return xc * lax.rsqrt(var + LN_EPS) * g + b


def _inproj_kernel(x_ref, w_ref, cos_ref, sin_ref, bgate_ref, qn_ref, kn_ref, lng_ref, lnb_ref,
                   sw_ref, sb_ref,
                   rq_ref, rk_ref, rv_ref, rg_ref, aq_ref, ak_ref, av_ref, ysgu_ref, gates_ref):
    tb = x_ref.shape[0]
    xb = x_ref[...].astype(BF16)
    cos2 = cos_ref[...]
    sin2 = sin_ref[...]

    def proj(i):
        return jnp.dot(xb, w_ref[:, OFFS[i]:OFFS[i + 1]], preferred_element_type=F32)

    h = proj(0)
    for hd in range(RET_HEADS):
        sl = slice(hd * RET_DK, (hd + 1) * RET_DK)
        rq_ref[:, sl] = _rope(h[:, sl], cos2, sin2).astype(BF16)
    h = proj(1)
    for hd in range(RET_HEADS):
        sl = slice(hd * RET_DK, (hd + 1) * RET_DK)
        rk_ref[:, sl] = (_rope(h[:, sl], cos2, sin2) * (RET_DK ** -0.5)).astype(BF16)
    rv_ref[...] = proj(2).astype(BF16)
    h = proj(3)
    rg_ref[...] = (h * jax.nn.sigmoid(h)).astype(BF16)

    h = proj(4)
    qn = qn_ref[...]
    for hd in range(ATT_HEADS):
        sl = slice(hd * ATT_DH, (hd + 1) * ATT_DH)
        t = h[:, sl]
        t = t * lax.rsqrt(jnp.mean(t * t, axis=-1, keepdims=True) + RMS_EPS) * qn
        aq_ref[:, sl] = (_rope(t, cos2, sin2) * ATT_Q_SCALE).astype(BF16)
    h = proj(5)
    kn = kn_ref[...]
    for hd in range(ATT_KV_HEADS):
        sl = slice(hd * ATT_DH, (hd + 1) * ATT_DH)
        t = h[:, sl]
        t = t * lax.rsqrt(jnp.mean(t * t, axis=-1, keepdims=True) + RMS_EPS) * kn
        ak_ref[:, sl] = _rope(t, cos2, sin2).astype(BF16)
    av_ref[...] = proj(6).astype(BF16)

    u = _gelu(proj(7))
    vn = _layer_norm_rows(_gelu(proj(8)), lng_ref[...], lnb_ref[...]).astype(BF16)
    sb = sb_ref[...]
    for c in range(tb // CHUNK):
        rows = slice(c * CHUNK, (c + 1) * CHUNK)
        for g in range(SGU_GROUPS):
            cols = slice(g * SGU_GW, (g + 1) * SGU_GW)
            mixed = jnp.dot(sw_ref[g], vn[rows, cols], preferred_element_type=F32) + sb[:, g:g + 1]
            ysgu_ref[rows, cols] = (u[rows, cols] * mixed).astype(BF16)

    gates_ref[...] = jax.nn.sigmoid(proj(9) + bgate_ref[...]).astype(BF16)


def _inproj(x2d, seq, w_bf, cos2, sin2, b_gate, qn, kn, lng, lnb, sgu_w_bf, sgu_bt, *, tb=256):
    t = x2d.shape[0]
    nblk_seq = seq // tb
    const = lambda i: (0, 0)
    row = lambda i: (i, 0)
    pos = lambda i: (i % nblk_seq, 0)
    outs = [(RET_QK_W, BF16), (RET_QK_W, BF16), (RET_V_W, BF16), (RET_V_W, BF16), (ATT_Q_W, BF16),
            (ATT_KV_W, BF16), (ATT_KV_W, BF16), (SGU_WIDTH, BF16), (GATE_W, BF16)]
    return pl.pallas_call(
        _inproj_kernel,
        grid=(t // tb,),
        in_specs=[
            pl.BlockSpec((tb, D_MODEL), row),
            pl.BlockSpec((D_MODEL, IN_W), const, pipeline_mode=pl.Buffered(1)),
            pl.BlockSpec((tb, ATT_DH), pos),
            pl.BlockSpec((tb, ATT_DH), pos),
            pl.BlockSpec((1, GATE_W), const),
            pl.BlockSpec((1, ATT_DH), const),
            pl.BlockSpec((1, ATT_DH), const),
            pl.BlockSpec((1, SGU_WIDTH), const),
            pl.BlockSpec((1, SGU_WIDTH), const),
            pl.BlockSpec((SGU_GROUPS, CHUNK, CHUNK), lambda i: (0, 0, 0)),
            pl.BlockSpec((CHUNK, SGU_GROUPS), const),
        ],
        out_specs=[pl.BlockSpec((tb, w), row) for w, _ in outs],
        out_shape=[jax.ShapeDtypeStruct((t, w), d) for w, d in outs],
        compiler_params=pltpu.CompilerParams(dimension_semantics=("arbitrary",),
                                             vmem_limit_bytes=VMEM_LIMIT_BYTES),
        name="inproj",
    )(x2d, w_bf, cos2, sin2, b_gate, qn, kn, lng, lnb, sgu_w_bf, sgu_bt)


def _retention_kernel(lg_ref, q_ref, k_ref, v_ref, *rest, rb, forward):
    if forward:
        g_ref, yb_ref, o_ref, state_ref = rest
    else:
        o_ref, state_ref = rest
    nchunk = rb // CHUNK

    @pl.when(pl.program_id(1) == 0)
    def _():
        state_ref[...] = jnp.zeros_like(state_ref)

    n_i = lax.broadcasted_iota(jnp.int32, (CHUNK, CHUNK), 0)
    m_i = lax.broadcasted_iota(jnp.int32, (CHUNK, CHUNK), 1)
    col = lax.broadcasted_iota(jnp.int32, (CHUNK, 1), 0).astype(F32)
    dist = ((n_i - m_i) if forward else (m_i - n_i)).astype(F32)
    live = (n_i >= m_i) if forward else (m_i > n_i)

    for hd in range(RET_HEADS):
        lg = lg_ref[0 if forward else 1, hd]
        decay = jnp.where(live, jnp.exp(lg * jnp.maximum(dist, 0.0)), 0.0)
        zeta = jnp.exp(lg * ((CHUNK - 1.0 - col) if forward else col))
        xi = jnp.exp(lg * ((col + 1.0) if forward else (CHUNK - col)))
        chunk_decay = jnp.exp(lg * CHUNK)
        qk_cols = slice(hd * RET_DK, (hd + 1) * RET_DK)
        v_cols = slice(hd * RET_DV, (hd + 1) * RET_DV)
        for c in (range(nchunk) if forward else reversed(range(nchunk))):
            rows = slice(c * CHUNK, (c + 1) * CHUNK)
            q = q_ref[rows, qk_cols]
            k = k_ref[rows, qk_cols]
            v = v_ref[rows, v_cols]
            sc = lax.dot_general(q, k, (((1,), (1,)), ((), ())), preferred_element_type=F32) * decay
            y = jnp.dot(sc.astype(BF16), v, preferred_element_type=F32)
            st = state_ref[hd]
            y = y + jnp.dot(q, st.astype(BF16), preferred_element_type=F32) * xi
            vz = (v.astype(F32) * zeta).astype(BF16)
            upd = lax.dot_general(k, vz, (((0,), (0,)), ((), ())), preferred_element_type=F32)
            state_ref[hd] = st * chunk_decay + upd
            if forward:
                y = y + yb_ref[rows, v_cols]
                mu = jnp.mean(y, axis=-1, keepdims=True)
                yc = y - mu
                var = jnp.mean(yc * yc, axis=-1, keepdims=True)
                yn = yc * lax.rsqrt(var + LN_EPS)
                o_ref[rows, v_cols] = (g_ref[rows, v_cols].astype(F32) * yn).astype(BF16)
            else:
                o_ref[rows, v_cols] = y


def _retention_pass(lg, rq, rk, rv, extra, batch, seq, *, rb, forward):
    t = rq.shape[0]
    ns = seq // rb
    if forward:
        blk = lambda b, s, lg_ref: (b * ns + s, 0)
    else:
        blk = lambda b, s, lg_ref: (b * ns + ns - 1 - s, 0)
    grid_spec = pltpu.PrefetchScalarGridSpec(
        num_scalar_prefetch=1,
        grid=(batch, ns),
        in_specs=[pl.BlockSpec((rb, RET_QK_W), blk), pl.BlockSpec((rb, RET_QK_W), blk),
                  pl.BlockSpec((rb, RET_V_W), blk)] + [pl.BlockSpec((rb, RET_V_W), blk)] * len(extra),
        out_specs=pl.BlockSpec((rb, RET_V_W), blk),
        scratch_shapes=[pltpu.VMEM((RET_HEADS, RET_DK, RET_DV), F32)],
    )
    return pl.pallas_call(
        functools.partial(_retention_kernel, rb=rb, forward=forward),
        grid_spec=grid_spec,
        out_shape=jax.ShapeDtypeStruct((t, RET_V_W), BF16 if forward else F32),
        compiler_params=pltpu.CompilerParams(dimension_semantics=("arbitrary", "arbitrary"),
                                             vmem_limit_bytes=VMEM_LIMIT_BYTES),
        name="retention_fwd" if forward else "retention_bwd",
    )(lg, rq, rk, rv, *extra)


def _retention(rq, rk, rv, rg, lg, batch, seq, *, rb=512):
    y_bwd = _retention_pass(lg, rq, rk, rv, (), batch, seq, rb=rb, forward=False)
    return _retention_pass(lg, rq, rk, rv, (rg, y_bwd), batch, seq, rb=rb, forward=True)


def _attention_kernel(q_ref, k_ref, vt_ref, o_ref, m_ref, l_ref, acc_ref):
    ki = pl.program_id(3)

    @pl.when(ki == 0)
    def _():
        m_ref[...] = jnp.full_like(m_ref, NEG_INF)
        l_ref[...] = jnp.zeros_like(l_ref)
        acc_ref[...] = jnp.zeros_like(acc_ref)

    k = k_ref[...]
    vt = vt_ref[...]
    def scores(g):
        q = q_ref[:, g * ATT_DH:(g + 1) * ATT_DH]
        return lax.dot_general(k, q, (((1,), (1,)), ((), ())), preferred_element_type=F32)

    st_next = scores(0)
    for g in range(ATT_GROUP):
        st = st_next
        if g + 1 < ATT_GROUP:
            st_next = scores(g + 1)
        m_prev = m_ref[g:g + 1, :]
        m_new = jnp.maximum(m_prev, jnp.max(st, axis=0, keepdims=True))
        alpha = jnp.exp2(m_prev - m_new)
        pt = jnp.exp2(st - m_new)
        l_ref[g:g + 1, :] = alpha * l_ref[g:g + 1, :] + jnp.sum(pt, axis=0, keepdims=True)
        acc_ref[g] = alpha * acc_ref[g] + jnp.dot(vt, pt.astype(BF16), preferred_element_type=F32)
        m_ref[g:g + 1, :] = m_new

    @pl.when(ki == pl.num_programs(3) - 1)
    def _():
        for g in range(ATT_GROUP):
            out = acc_ref[g] / l_ref[g:g + 1, :]
            o_ref[:, g * ATT_DH:(g + 1) * ATT_DH] = out.T.astype(BF16)


def _attention(aq, ak, avt, batch, seq, *, tq=512, tk=4096):
    t = aq.shape[0]
    nq = seq // tq
    nk = seq // tk
    gw = ATT_GROUP * ATT_DH
    return pl.pallas_call(
        _attention_kernel,
        grid=(batch, ATT_KV_HEADS, nq, nk),
        in_specs=[
            pl.BlockSpec((tq, gw), lambda b, g, qi, ki: (b * nq + qi, g)),
            pl.BlockSpec((tk, ATT_DH), lambda b, g, qi, ki: (b * nk + ki, g)),
            pl.BlockSpec((ATT_DH, tk), lambda b, g, qi, ki: (g, b * nk + ki)),
        ],
        out_specs=pl.BlockSpec((tq, gw), lambda b, g, qi, ki: (b * nq + qi, g)),
        out_shape=jax.ShapeDtypeStruct((t, ATT_Q_W), BF16),
        scratch_shapes=[
            pltpu.VMEM((SUBLANES, tq), F32),
            pltpu.VMEM((SUBLANES, tq), F32),
            pltpu.VMEM((ATT_GROUP, ATT_DH, tq), F32),
        ],
        compiler_params=pltpu.CompilerParams(
            dimension_semantics=("arbitrary", "arbitrary", "arbitrary", "arbitrary"),
            vmem_limit_bytes=VMEM_LIMIT_BYTES),
        name="attention",
    )(aq, ak, avt)


def _merge_kernel(x_ref, yr_ref, ya_ref, ys_ref, gates_ref, wb_ref, wo_ref, g_ref, b_ref, o_ref):
    merged = None
    for i, y_ref in enumerate((yr_ref, ya_ref, ys_ref)):
        gate = gates_ref[:, i * D_MODEL:(i + 1) * D_MODEL].astype(F32)
        term = gate * jnp.dot(y_ref[...], wb_ref[i], preferred_element_type=F32)
        merged = term if merged is None else merged + term
    proj = jnp.dot(merged.astype(BF16), wo_ref[...], preferred_element_type=F32)
    o_ref[...] = _layer_norm_rows(DEEPNORM_ALPHA * x_ref[...] + proj, g_ref[...], b_ref[...])


def _merge(x2d, y_ret, y_att, y_sgu, gates, wb_bf, wo_bf, ln_g, ln_b, *, tb=512):
    t = x2d.shape[0]
    row = lambda i: (i, 0)
    const = lambda i: (0, 0)
    return pl.pallas_call(
        _merge_kernel,
        grid=(t // tb,),
        in_specs=[
            pl.BlockSpec((tb, D_MODEL), row),
            pl.BlockSpec((tb, D_MODEL), row),
            pl.BlockSpec((tb, D_MODEL), row),
            pl.BlockSpec((tb, D_MODEL), row),
            pl.BlockSpec((tb, GATE_W), row),
            pl.BlockSpec((N_BRANCH, D_MODEL, D_MODEL), lambda i: (0, 0, 0)),
            pl.BlockSpec((D_MODEL, D_MODEL), const),
            pl.BlockSpec((1, D_MODEL), const),
            pl.BlockSpec((1, D_MODEL), const),
        ],
        out_specs=pl.BlockSpec((tb, D_MODEL), row),
        out_shape=jax.ShapeDtypeStruct((t, D_MODEL), F32),
        compiler_params=pltpu.CompilerParams(dimension_semantics=("arbitrary",),
                                             vmem_limit_bytes=VMEM_LIMIT_BYTES),
        name="merge",
    )(x2d, y_ret, y_att, y_sgu, gates, wb_bf, wo_bf, ln_g, ln_b)


PEER_PAD_ID = 1024.0


def _peer_candidate_ids():
    pairs = [(0, b) for b in range(PEER_TOPK)]
    pairs += [(a, b) for a in range(1, SUBLANES) for b in range(SUBLANES)]
    pairs += [(a, 0) for a in range(SUBLANES, PEER_TOPK)]
    ids = [float(a * PEER_TOPK + b) if (a + 1) * (b + 1) <= PEER_TOPK else PEER_PAD_ID for a, b in pairs]
    return np.asarray(ids, np.float32)


def _topk_rows(v, ids, k):
    vals, idxs = [], []
    for _ in range(k):
        m = jnp.max(v, axis=0, keepdims=True)
        idx = jnp.min(jnp.where(v == m, ids, 2.0 * PEER_PAD_ID), axis=0, keepdims=True)
        vals.append(m)
        idxs.append(idx)
        v = jnp.where(ids == idx, NEG_INF, v)
    return vals, idxs


def _route_kernel(x_ref, wq_ref, keys_ref, cid_ref, i1_ref, i2_ref, gate_ref):
    tb = x_ref.shape[0]
    q = jnp.dot(x_ref[...].astype(BF16), wq_ref[...], preferred_element_type=F32).astype(BF16)
    half = PEER_DQ // 2
    key_row = lax.broadcasted_iota(jnp.int32, (PEER_NKEYS, tb), 0).astype(F32)
    arow = lax.broadcasted_iota(jnp.int32, (PEER_TOPK, tb), 0).astype(F32)
    cand_ids = cid_ref[...]
    i1_rows, i2_rows, gate_rows = [], [], []
    for h in range(PEER_HEADS):
        sub = []
        for p in range(2):
            qs = q[:, (h * 2 + p) * half:(h * 2 + p + 1) * half]
            sc = lax.dot_general(keys_ref[h * 2 + p], qs, (((1,), (1,)), ((), ())),
                                 preferred_element_type=F32)
            sub.append(_topk_rows(sc, key_row, PEER_TOPK))
        (s1, i1), (s2, i2) = sub
        s1m = jnp.concatenate(s1, axis=0)
        s2m = jnp.concatenate(s2, axis=0)
        i1m = jnp.concatenate(i1, axis=0)
        i2m = jnp.concatenate(i2, axis=0)
        pieces = [s1[0] + s2m[:SUBLANES], s1[0] + s2m[SUBLANES:]]
        pieces += [s1[a] + s2m[:SUBLANES] for a in range(1, SUBLANES)]
        pieces += [s1m[SUBLANES:] + s2[0]]
        cand = jnp.where(cand_ids < PEER_PAD_ID, jnp.concatenate(pieces, axis=0), NEG_INF)
        top_s, top_pos = _topk_rows(cand, cand_ids, PEER_TOPK)
        top_s = jnp.concatenate(top_s, axis=0)
        e = jnp.exp(top_s - top_s[0:1, :])
        gate_rows.append(e / jnp.sum(e, axis=0, keepdims=True))
        for pos in top_pos:
            a = jnp.floor(pos * (1.0 / PEER_TOPK))
            b = pos - a * PEER_TOPK
            i1_rows.append(jnp.sum(jnp.where(arow == a, i1m, 0.0), axis=0, keepdims=True))
            i2_rows.append(jnp.sum(jnp.where(arow == b, i2m, 0.0), axis=0, keepdims=True))
    i1_ref[...] = jnp.concatenate(i1_rows, axis=0).T
    i2_ref[...] = jnp.concatenate(i2_rows, axis=0).T
    gate_ref[...] = jnp.concatenate(gate_rows, axis=0).T


def _route(x1, wq_bf, keys_bf, *, tb=128):
    t = x1.shape[0]
    hk = PEER_HEADS * PEER_TOPK
    row = lambda i: (i, 0)
    ids = _peer_candidate_ids()
    cand_ids = jnp.asarray(np.ascontiguousarray(np.broadcast_to(ids[:, None], (ids.shape[0], tb))))
    return pl.pallas_call(
        _route_kernel,
        grid=(t // tb,),
        in_specs=[
            pl.BlockSpec((tb, D_MODEL), row),
            pl.BlockSpec((D_MODEL, PEER_HEADS * PEER_DQ), lambda i: (0, 0)),
            pl.BlockSpec((PEER_HEADS * 2, PEER_NKEYS, PEER_DQ // 2), lambda i: (0, 0, 0)),
            pl.BlockSpec(cand_ids.shape, lambda i: (0, 0)),
        ],
        out_specs=[pl.BlockSpec((tb, hk), row)] * 3,
        out_shape=[jax.ShapeDtypeStruct((t, hk), F32)] * 3,
        compiler_params=pltpu.CompilerParams(dimension_semantics=("arbitrary",),
                                             vmem_limit_bytes=VMEM_LIMIT_BYTES),
        name="peer_route",
    )(x1, wq_bf, keys_bf, cand_ids)


def _fp8_scale(bound):
    return jnp.exp2(jnp.floor(jnp.log2(FP8_TARGET / jnp.maximum(bound, FP8_TINY))))


def _peer_kernel(x_ref, i1_ref, i2_ref, gate_ref, ut_ref, stat_ref, v_ref, g_ref, b_ref, o_ref,
                 gs_ref, x8_ref, hscale_ref, wscale_ref, acc_ref, *, tb, ec, pitch, unroll):
    e = pl.program_id(1)
    ncl = ec // PEER_NKEYS

    @pl.when(e == 0)
    def _():
        x = x_ref[...]
        sx = _fp8_scale(jnp.max(jnp.max(jnp.abs(x), axis=0, keepdims=True), axis=1, keepdims=True))
        x8_ref[...] = (x * sx).astype(FP8)
        hscale_ref[...] = jnp.broadcast_to(stat_ref[0:1, 0:1] / sx, hscale_ref.shape)
        row_norm = jnp.sqrt(jnp.sum(x * x, axis=1, keepdims=True))
        sw = _fp8_scale(PEER_HEADS * row_norm * stat_ref[0:1, 1:2])
        wscale_ref[...] = jnp.broadcast_to(sw, wscale_ref.shape)
        acc_ref[...] = jnp.zeros_like(acc_ref)
        key_id = lax.broadcasted_iota(jnp.int32, (PEER_NKEYS, PEER_NKEYS), 0).astype(F32)

        def build(blk, carry):
            for r in range(unroll):
                t = blk * unroll + r
                i1 = jnp.broadcast_to(i1_ref[pl.ds(t, 1), :], (PEER_NKEYS, PEER_NKEYS))
                i2 = jnp.broadcast_to(i2_ref[pl.ds(t, 1), :], (PEER_NKEYS, PEER_NKEYS))
                gt = jnp.broadcast_to(0.5 * gate_ref[pl.ds(t, 1), :] * wscale_ref[pl.ds(t, 1), :]
                                      * hscale_ref[0:1, :], (PEER_NKEYS, PEER_NKEYS))
                a_t = jnp.where(i1 == key_id, gt, 0.0).astype(BF16)
                b_t = jnp.where(i2 == key_id, 1.0, 0.0).astype(BF16)
                g_tok = lax.dot_general(a_t, b_t, (((1,), (1,)), ((), ())), preferred_element_type=F32)
                g_rnd = g_tok.astype(BF16).astype(F32).reshape(PEER_NKEYS // PEER_PAIR, PEER_PAIR, PEER_NKEYS)
                hi = lax.bitcast_convert_type(g_rnd[:, :SUBLANES, :].reshape(PEER_PLANES, PEER_NKEYS), jnp.uint32)
                lo = lax.bitcast_convert_type(g_rnd[:, SUBLANES:, :].reshape(PEER_PLANES, PEER_NKEYS),
                                              jnp.uint32) >> 16
                gs_ref[pl.ds(t, PEER_PLANES, stride=pitch), :] = hi | lo
            return carry

        lax.fori_loop(0, tb // unroll, build, 0)

    d = jnp.dot(x8_ref[...], ut_ref[...], preferred_element_type=F32)
    act = d * (1.0 + lax.erf(d * (hscale_ref[0:1, 0:1] * 2.0 ** -0.5)))
    parts = [None] * ncl
    for grp in range(ncl // PEER_PAIR):
        for i in range(SUBLANES):
            plane = (e * (ncl // PEER_PAIR) + grp) * SUBLANES + i
            word = gs_ref[pl.ds(pl.multiple_of(plane * pitch, SUBLANES), tb), :]
            halves = (lax.bitcast_convert_type(word & jnp.uint32(0xFFFF0000), F32),
                      lax.bitcast_convert_type(word << 16, F32))
            for half, gval in enumerate(halves):
                cl = grp * PEER_PAIR + half * SUBLANES + i
                parts[cl] = act[:, cl * PEER_NKEYS:(cl + 1) * PEER_NKEYS] * gval
    w = jnp.concatenate(parts, axis=1).astype(FP8)
    acc_ref[...] += jnp.dot(w, v_ref[...], preferred_element_type=F32)

    @pl.when(e == pl.num_programs(1) - 1)
    def _():
        peer = acc_ref[...] * (stat_ref[0:1, 2:3] / wscale_ref[:, 0:1])
        o_ref[...] = _layer_norm_rows(DEEPNORM_ALPHA * x_ref[...] + peer, g_ref[...], b_ref[...])


def _peer(x1, i1, i2, gate, ut_f8, stats, v_f8, ln_g, ln_b, *, tb=512, ec=2048, unroll=32):
    t = x1.shape[0]
    n_exp = v_f8.shape[0]
    hk = PEER_HEADS * PEER_TOPK
    pitch = tb + SUBLANES
    row = lambda i, e: (i, 0)
    const = lambda i, e: (0, 0)
    return pl.pallas_call(
        functools.partial(_peer_kernel, tb=tb, ec=ec, pitch=pitch, unroll=unroll),
        grid=(t // tb, n_exp // ec),
        in_specs=[
            pl.BlockSpec((tb, D_MODEL), row),
            pl.BlockSpec((tb, hk), row),
            pl.BlockSpec((tb, hk), row),
            pl.BlockSpec((tb, hk), row),
            pl.BlockSpec((D_MODEL, ec), lambda i, e: (0, e)),
            pl.BlockSpec((1, LANES), const),
            pl.BlockSpec((ec, D_MODEL), lambda i, e: (e, 0)),
            pl.BlockSpec((1, D_MODEL), const),
            pl.BlockSpec((1, D_MODEL), const),
        ],
        out_specs=pl.BlockSpec((tb, D_MODEL), row),
        out_shape=jax.ShapeDtypeStruct((t, D_MODEL), F32),
        scratch_shapes=[
            pltpu.VMEM((PEER_PLANES * pitch, PEER_NKEYS), jnp.uint32),
            pltpu.VMEM((tb, D_MODEL), FP8),
            pltpu.VMEM((SUBLANES, LANES), F32),
            pltpu.VMEM((tb, LANES), F32),
            pltpu.VMEM((tb, D_MODEL), F32),
        ],
        compiler_params=pltpu.CompilerParams(dimension_semantics=("arbitrary", "arbitrary"),
                                             vmem_limit_bytes=VMEM_LIMIT_BYTES),
        name="peer_dense",
    )(x1, i1, i2, gate, ut_f8, stats, v_f8, ln_g, ln_b)


def _rope_tables(seq):
    rows = seq // GRID_W
    row = jnp.repeat(jnp.arange(rows, dtype=F32), GRID_W)
    col = jnp.tile(jnp.arange(GRID_W, dtype=F32), rows)
    n_freq = ATT_DH // 4
    inv_freq = ROPE_THETA ** (-jnp.arange(n_freq, dtype=F32) / n_freq)
    ang = jnp.concatenate([row[:, None] * inv_freq, col[:, None] * inv_freq], axis=-1)
    cos, sin = jnp.cos(ang), jnp.sin(ang)
    return jnp.concatenate([cos, cos], axis=-1), jnp.concatenate([-sin, sin], axis=-1)


def _layer(x2d, batch, seq, cos2, sin2, p):
    rq, rk, rv, rg, aq, ak, av, y_sgu, gates = _inproj(
        x2d, seq, p["w_in"], cos2, sin2, p["b_gate"], p["qn"], p["kn"], p["sgu_ln_g"], p["sgu_ln_b"],
        p["sgu_w"], p["sgu_bt"])
    y_ret = _retention(rq, rk, rv, rg, p["lg"], batch, seq)
    y_att = _attention(aq, ak, jnp.transpose(av), batch, seq)
    x1 = _merge(x2d, y_ret, y_att, y_sgu, gates, p["w_branch"], p["w_out"], p["ln1_g"], p["ln1_b"])
    i1, i2, gate = _route(x1, p["peer_wq"], p["peer_keys"])
    return _peer(x1, i1, i2, gate, p["peer_ut"], p["peer_stats"], p["peer_v"], p["ln2_g"], p["ln2_b"])


def _prep_layer(l, w_in, b_gate, ret_decay_fwd, ret_decay_bwd, attn_q_norm, attn_k_norm,
                sgu_ln_g, sgu_ln_b, sgu_w, sgu_b, w_branch, w_out, ln1_g, ln1_b,
                peer_wq, peer_keys, peer_u, peer_v, ln2_g, ln2_b):
    half = PEER_DQ // 2
    u_scale = _fp8_scale(jnp.max(jnp.abs(peer_u[l])))
    v_scale = _fp8_scale(jnp.max(jnp.abs(peer_v[l])))
    u_norm = jnp.sqrt(jnp.max(jnp.sum(jnp.square(peer_u[l]), axis=1)))
    stats = jnp.zeros((1, LANES), F32).at[0, 0].set(1.0 / u_scale).at[0, 1].set(u_norm).at[0, 2].set(1.0 / v_scale)
    return dict(
        w_in=w_in[l].astype(BF16),
        b_gate=b_gate[l].reshape(1, GATE_W),
        lg=jnp.stack([jax.nn.log_sigmoid(ret_decay_fwd[l].astype(F32)),
                      jax.nn.log_sigmoid(ret_decay_bwd[l].astype(F32))]),
        qn=attn_q_norm[l].reshape(1, ATT_DH),
        kn=attn_k_norm[l].reshape(1, ATT_DH),
        sgu_ln_g=sgu_ln_g[l].reshape(1, SGU_WIDTH),
        sgu_ln_b=sgu_ln_b[l].reshape(1, SGU_WIDTH),
        sgu_w=sgu_w[l].astype(BF16),
        sgu_bt=jnp.transpose(sgu_b[l]),
        w_branch=w_branch[l].astype(BF16),
        w_out=w_out[l].astype(BF16),
        ln1_g=ln1_g[l].reshape(1, D_MODEL),
        ln1_b=ln1_b[l].reshape(1, D_MODEL),
        peer_wq=peer_wq[l].astype(BF16),
        peer_keys=peer_keys[l].reshape(PEER_HEADS * 2, PEER_NKEYS, half).astype(BF16),
        peer_ut=jnp.transpose(peer_u[l] * u_scale).astype(FP8),
        peer_stats=stats,
        peer_v=(peer_v[l] * v_scale).astype(FP8),
        ln2_g=ln2_g[l].reshape(1, D_MODEL),
        ln2_b=ln2_b[l].reshape(1, D_MODEL),
    )


def kernel(x_prompt, x_sample, w_in, b_gate, ret_decay_fwd, ret_decay_bwd, attn_q_norm, attn_k_norm,
           sgu_ln_g, sgu_ln_b, sgu_w, sgu_b, w_branch, w_out, ln1_g, ln1_b,
           peer_wq, peer_keys, peer_u, peer_v, ln2_g, ln2_b):
    weights = (w_in, b_gate, ret_decay_fwd, ret_decay_bwd, attn_q_norm, attn_k_norm,
               sgu_ln_g, sgu_ln_b, sgu_w, sgu_b, w_branch, w_out, ln1_g, ln1_b,
               peer_wq, peer_keys, peer_u, peer_v, ln2_g, ln2_b)
    layers = [_prep_layer(l, *weights) for l in range(DEPTH)]
    outs = []
    for x in (x_prompt, x_sample):
        batch, seq, _ = x.shape
        cos2, sin2 = _rope_tables(seq)
        h = x.reshape(batch * seq, D_MODEL)
        for p in layers:
            h = _layer(h, batch, seq, cos2, sin2, p)
        outs.append(h.reshape(batch, seq, D_MODEL))
    return tuple(outs)
```

```python
import functools
import math

import jax
import jax.numpy as jnp
import numpy as np
from jax import lax
from jax.experimental import pallas as pl
from jax.experimental.pallas import tpu as pltpu

F32 = jnp.float32
BF16 = jnp.bfloat16
FP8 = jnp.float8_e4m3fn

D_MODEL = 1024
DEPTH = 2
GRID_W = 64
CHUNK = 128
RET_HEADS = 4
RET_DK = 128
RET_DV = 256
ATT_HEADS = 8
ATT_KV_HEADS = 2
ATT_DH = 128
ATT_GROUP = ATT_HEADS // ATT_KV_HEADS
SGU_GROUPS = 4
SGU_WIDTH = 1024
SGU_GW = SGU_WIDTH // SGU_GROUPS
PEER_HEADS = 8
PEER_NKEYS = 128
PEER_DQ = 256
PEER_TOPK = 16
PEER_PLANES = PEER_NKEYS // 2
PEER_PAIR = 16
N_BRANCH = 3
ROPE_THETA = 10000.0
DEEPNORM_ALPHA = (2 * DEPTH) ** 0.25
LN_EPS = 1e-5
RMS_EPS = 1e-6

RET_QK_W = RET_HEADS * RET_DK
RET_V_W = RET_HEADS * RET_DV
ATT_Q_W = ATT_HEADS * ATT_DH
ATT_KV_W = ATT_KV_HEADS * ATT_DH
GATE_W = N_BRANCH * D_MODEL
SPLITS = (RET_QK_W, RET_QK_W, RET_V_W, RET_V_W, ATT_Q_W, ATT_KV_W, ATT_KV_W, SGU_WIDTH, SGU_WIDTH, GATE_W)
OFFS = tuple(int(sum(SPLITS[:i])) for i in range(len(SPLITS) + 1))
IN_W = OFFS[-1]

LANES = 128
SUBLANES = 8
VMEM_LIMIT_BYTES = 56 * 1024 * 1024

NEG_INF = float("-inf")
FP8_TARGET = 224.0
FP8_TINY = 1e-30
ATT_Q_SCALE = ATT_DH ** -0.5 * math.log2(math.e)


def _gelu(x):
    return 0.5 * x * (1.0 + lax.erf(x * (2.0 ** -0.5)))


def _rope(x, cos2, sin2):
    return x * cos2 + pltpu.roll(x, ATT_DH // 2, 1) * sin2


def _layer_norm_rows(x, g, b):
    mu = jnp.mean(x, axis=-1, keepdims=True)
    xc = x - mu
    var = jnp.mean(xc * xc, axis=-1, keepdims=True)
    return xc * lax.rsqrt(var + LN_EPS) * g + b


def _inproj_kernel(x_ref, w_ref, cos_ref, sin_ref, bgate_ref, qn_ref, kn_ref, lng_ref, lnb_ref,
                   sw_ref, sb_ref,
                   rq_ref, rk_ref, rv_ref, rg_ref, aq_ref, ak_ref, av_ref, ysgu_ref, gates_ref):
    tb = x_ref.shape[0]
    xb = x_ref[...].astype(BF16)
    cos2 = cos_ref[...]
    sin2 = sin_ref[...]

    def proj(i):
        return jnp.dot(xb, w_ref[:, OFFS[i]:OFFS[i + 1]], preferred_element_type=F32)

    h = proj(0)
    for hd in range(RET_HEADS):
        sl = slice(hd * RET_DK, (hd + 1) * RET_DK)
        rq_ref[:, sl] = _rope(h[:, sl], cos2, sin2).astype(BF16)
    h = proj(1)
    for hd in range(RET_HEADS):
        sl = slice(hd * RET_DK, (hd + 1) * RET_DK)
        rk_ref[:, sl] = (_rope(h[:, sl], cos2, sin2) * (RET_DK ** -0.5)).astype(BF16)
    rv_ref[...] = proj(2).astype(BF16)
    h = proj(3)
    rg_ref[...] = (h * jax.nn.sigmoid(h)).astype(BF16)

    h = proj(4)
    qn = qn_ref[...]
    for hd in range(ATT_HEADS):
        sl = slice(hd * ATT_DH, (hd + 1) * ATT_DH)
        t = h[:, sl]
        t = t * lax.rsqrt(jnp.mean(t * t, axis=-1, keepdims=True) + RMS_EPS) * qn
        aq_ref[:, sl] = (_rope(t, cos2, sin2) * ATT_Q_SCALE).astype(BF16)
    h = proj(5)
    kn = kn_ref[...]
    for hd in range(ATT_KV_HEADS):
        sl = slice(hd * ATT_DH, (hd + 1) * ATT_DH)
        t = h[:, sl]
        t = t * lax.rsqrt(jnp.mean(t * t, axis=-1, keepdims=True) + RMS_EPS) * kn
        ak_ref[:, sl] = _rope(t, cos2, sin2).astype(BF16)
    av_ref[...] = proj(6).astype(BF16)

    u = _gelu(proj(7))
    vn = _layer_norm_rows(_gelu(proj(8)), lng_ref[...], lnb_ref[...]).astype(BF16)
    sb = sb_ref[...]
    for c in range(tb // CHUNK):
        rows = slice(c * CHUNK, (c + 1) * CHUNK)
        for g in range(SGU_GROUPS):
            cols = slice(g * SGU_GW, (g + 1) * SGU_GW)
            mixed = jnp.dot(sw_ref[g], vn[rows, cols], preferred_element_type=F32) + sb[:, g:g + 1]
            ysgu_ref[rows, cols] = (u[rows, cols] * mixed).astype(BF16)

    gates_ref[...] = jax.nn.sigmoid(proj(9) + bgate_ref[...]).astype(BF16)


def _inproj(x2d, seq, w_bf, cos2, sin2, b_gate, qn, kn, lng, lnb, sgu_w_bf, sgu_bt, *, tb=256):
    t = x2d.shape[0]
    nblk_seq = seq // tb
    const = lambda i: (0, 0)
    row = lambda i: (i, 0)
    pos = lambda i: (i % nblk_seq, 0)
    outs = [(RET_QK_W, BF16), (RET_QK_W, BF16), (RET_V_W, BF16), (RET_V_W, BF16), (ATT_Q_W, BF16),
            (ATT_KV_W, BF16), (ATT_KV_W, BF16), (SGU_WIDTH, BF16), (GATE_W, BF16)]
    return pl.pallas_call(
        _inproj_kernel,
        grid=(t // tb,),
        in_specs=[
            pl.BlockSpec((tb, D_MODEL), row),
            pl.BlockSpec((D_MODEL, IN_W), const, pipeline_mode=pl.Buffered(1)),
            pl.BlockSpec((tb, ATT_DH), pos),
            pl.BlockSpec((tb, ATT_DH), pos),
            pl.BlockSpec((1, GATE_W), const),
            pl.BlockSpec((1, ATT_DH), const),
            pl.BlockSpec((1, ATT_DH), const),
            pl.BlockSpec((1, SGU_WIDTH), const),
            pl.BlockSpec((1, SGU_WIDTH), const),
            pl.BlockSpec((SGU_GROUPS, CHUNK, CHUNK), lambda i: (0, 0, 0)),
            pl.BlockSpec((CHUNK, SGU_GROUPS), const),
        ],
        out_specs=[pl.BlockSpec((tb, w), row) for w, _ in outs],
        out_shape=[jax.ShapeDtypeStruct((t, w), d) for w, d in outs],
        compiler_params=pltpu.CompilerParams(dimension_semantics=("arbitrary",),
                                             vmem_limit_bytes=VMEM_LIMIT_BYTES),
        name="inproj",
    )(x2d, w_bf, cos2, sin2, b_gate, qn, kn, lng, lnb, sgu_w_bf, sgu_bt)


def _retention_kernel(lg_ref, q_ref, k_ref, v_ref, *rest, rb, forward):
    if forward:
        g_ref, yb_ref, o_ref, state_ref = rest
    else:
        o_ref, state_ref = rest
    nchunk = rb // CHUNK

    @pl.when(pl.program_id(1) == 0)
    def _():
        state_ref[...] = jnp.zeros_like(state_ref)

    n_i = lax.broadcasted_iota(jnp.int32, (CHUNK, CHUNK), 0)
    m_i = lax.broadcasted_iota(jnp.int32, (CHUNK, CHUNK), 1)
    col = lax.broadcasted_iota(jnp.int32, (CHUNK, 1), 0).astype(F32)
    dist = ((n_i - m_i) if forward else (m_i - n_i)).astype(F32)
    live = (n_i >= m_i) if forward else (m_i > n_i)

    for hd in range(RET_HEADS):
        lg = lg_ref[0 if forward else 1, hd]
        decay = jnp.where(live, jnp.exp(lg * jnp.maximum(dist, 0.0)), 0.0)
        zeta = jnp.exp(lg * ((CHUNK - 1.0 - col) if forward else col))
        xi = jnp.exp(lg * ((col + 1.0) if forward else (CHUNK - col)))
        chunk_decay = jnp.exp(lg * CHUNK)
        qk_cols = slice(hd * RET_DK, (hd + 1) * RET_DK)
        v_cols = slice(hd * RET_DV, (hd + 1) * RET_DV)
        for c in (range(nchunk) if forward else reversed(range(nchunk))):
            rows = slice(c * CHUNK, (c + 1) * CHUNK)
            q = q_ref[rows, qk_cols]
            k = k_ref[rows, qk_cols]
            v = v_ref[rows, v_cols]
            sc = lax.dot_general(q, k, (((1,), (1,)), ((), ())), preferred_element_type=F32) * decay
            y = jnp.dot(sc.astype(BF16), v, preferred_element_type=F32)
            st = state_ref[hd]
            y = y + jnp.dot(q, st.astype(BF16), preferred_element_type=F32) * xi
            vz = (v.astype(F32) * zeta).astype(BF16)
            upd = lax.dot_general(k, vz, (((0,), (0,)), ((), ())), preferred_element_type=F32)
            state_ref[hd] = st * chunk_decay + upd
            if forward:
                y = y + yb_ref[rows, v_cols]
                mu = jnp.mean(y, axis=-1, keepdims=True)
                yc = y - mu
                var = jnp.mean(yc * yc, axis=-1, keepdims=True)
                yn = yc * lax.rsqrt(var + LN_EPS)
                o_ref[rows, v_cols] = (g_ref[rows, v_cols].astype(F32) * yn).astype(BF16)
            else:
                o_ref[rows, v_cols] = y


def _retention_pass(lg, rq, rk, rv, extra, batch, seq, *, rb, forward):
    t = rq.shape[0]
    ns = seq // rb
    if forward:
        blk = lambda b, s, lg_ref: (b * ns + s, 0)
    else:
        blk = lambda b, s, lg_ref: (b * ns + ns - 1 - s, 0)
    grid_spec = pltpu.PrefetchScalarGridSpec(
        num_scalar_prefetch=1,
        grid=(batch, ns),
        in_specs=[pl.BlockSpec((rb, RET_QK_W), blk), pl.BlockSpec((rb, RET_QK_W), blk),
                  pl.BlockSpec((rb, RET_V_W), blk)] + [pl.BlockSpec((rb, RET_V_W), blk)] * len(extra),
        out_specs=pl.BlockSpec((rb, RET_V_W), blk),
        scratch_shapes=[pltpu.VMEM((RET_HEADS, RET_DK, RET_DV), F32)],
    )
    return pl.pallas_call(
        functools.partial(_retention_kernel, rb=rb, forward=forward),
        grid_spec=grid_spec,
        out_shape=jax.ShapeDtypeStruct((t, RET_V_W), BF16 if forward else F32),
        compiler_params=pltpu.CompilerParams(dimension_semantics=("arbitrary", "arbitrary"),
                                             vmem_limit_bytes=VMEM_LIMIT_BYTES),
        name="retention_fwd" if forward else "retention_bwd",
    )(lg, rq, rk, rv, *extra)


def _retention(rq, rk, rv, rg, lg, batch, seq, *, rb=512):
    y_bwd = _retention_pass(lg, rq, rk, rv, (), batch, seq, rb=rb, forward=False)
    return _retention_pass(lg, rq, rk, rv, (rg, y_bwd), batch, seq, rb=rb, forward=True)


def _attention_kernel(q_ref, k_ref, vt_ref, o_ref, m_ref, l_ref, acc_ref):
    ki = pl.program_id(3)

    @pl.when(ki == 0)
    def _():
        m_ref[...] = jnp.full_like(m_ref, NEG_INF)
        l_ref[...] = jnp.zeros_like(l_ref)
        acc_ref[...] = jnp.zeros_like(acc_ref)

    k = k_ref[...]
    vt = vt_ref[...]
    def scores(g):
        q = q_ref[:, g * ATT_DH:(g + 1) * ATT_DH]
        return lax.dot_general(k, q, (((1,), (1,)), ((), ())), preferred_element_type=F32)

    st_next = scores(0)
    for g in range(ATT_GROUP):
        st = st_next
        if g + 1 < ATT_GROUP:
            st_next = scores(g + 1)
        m_prev = m_ref[g:g + 1, :]
        m_new = jnp.maximum(m_prev, jnp.max(st, axis=0, keepdims=True))
        alpha = jnp.exp2(m_prev - m_new)
        pt = jnp.exp2(st - m_new)
        l_ref[g:g + 1, :] = alpha * l_ref[g:g + 1, :] + jnp.sum(pt, axis=0, keepdims=True)
        acc_ref[g] = alpha * acc_ref[g] + jnp.dot(vt, pt.astype(BF16), preferred_element_type=F32)
        m_ref[g:g + 1, :] = m_new

    @pl.when(ki == pl.num_programs(3) - 1)
    def _():
        for g in range(ATT_GROUP):
            out = acc_ref[g] / l_ref[g:g + 1, :]
            o_ref[:, g * ATT_DH:(g + 1) * ATT_DH] = out.T.astype(BF16)


def _attention(aq, ak, avt, batch, seq, *, tq=512, tk=4096):
    t = aq.shape[0]
    nq = seq // tq
    nk = seq // tk
    gw = ATT_GROUP * ATT_DH
    return pl.pallas_call(
        _attention_kernel,
        grid=(batch, ATT_KV_HEADS, nq, nk),
        in_specs=[
            pl.BlockSpec((tq, gw), lambda b, g, qi, ki: (b * nq + qi, g)),
            pl.BlockSpec((tk, ATT_DH), lambda b, g, qi, ki: (b * nk + ki, g)),
            pl.BlockSpec((ATT_DH, tk), lambda b, g, qi, ki: (g, b * nk + ki)),
        ],
        out_specs=pl.BlockSpec((tq, gw), lambda b, g, qi, ki: (b * nq + qi, g)),
        out_shape=jax.ShapeDtypeStruct((t, ATT_Q_W), BF16),
        scratch_shapes=[
            pltpu.VMEM((SUBLANES, tq), F32),
            pltpu.VMEM((SUBLANES, tq), F32),
            pltpu.VMEM((ATT_GROUP, ATT_DH, tq), F32),
        ],
        compiler_params=pltpu.CompilerParams(
            dimension_semantics=("arbitrary", "arbitrary", "arbitrary", "arbitrary"),
            vmem_limit_bytes=VMEM_LIMIT_BYTES),
        name="attention",
    )(aq, ak, avt)


def _merge_kernel(x_ref, yr_ref, ya_ref, ys_ref, gates_ref, wb_ref, wo_ref, g_ref, b_ref, o_ref):
    merged = None
    for i, y_ref in enumerate((yr_ref, ya_ref, ys_ref)):
        gate = gates_ref[:, i * D_MODEL:(i + 1) * D_MODEL].astype(F32)
        term = gate * jnp.dot(y_ref[...], wb_ref[i], preferred_element_type=F32)
        merged = term if merged is None else merged + term
    proj = jnp.dot(merged.astype(BF16), wo_ref[...], preferred_element_type=F32)
    o_ref[...] = _layer_norm_rows(DEEPNORM_ALPHA * x_ref[...] + proj, g_ref[...], b_ref[...])


def _merge(x2d, y_ret, y_att, y_sgu, gates, wb_bf, wo_bf, ln_g, ln_b, *, tb=512):
    t = x2d.shape[0]
    row = lambda i: (i, 0)
    const = lambda i: (0, 0)
    return pl.pallas_call(
        _merge_kernel,
        grid=(t // tb,),
        in_specs=[
            pl.BlockSpec((tb, D_MODEL), row),
            pl.BlockSpec((tb, D_MODEL), row),
            pl.BlockSpec((tb, D_MODEL), row),
            pl.BlockSpec((tb, D_MODEL), row),
            pl.BlockSpec((tb, GATE_W), row),
            pl.BlockSpec((N_BRANCH, D_MODEL, D_MODEL), lambda i: (0, 0, 0)),
            pl.BlockSpec((D_MODEL, D_MODEL), const),
            pl.BlockSpec((1, D_MODEL), const),
            pl.BlockSpec((1, D_MODEL), const),
        ],
        out_specs=pl.BlockSpec((tb, D_MODEL), row),
        out_shape=jax.ShapeDtypeStruct((t, D_MODEL), F32),
        compiler_params=pltpu.CompilerParams(dimension_semantics=("arbitrary",),
                                             vmem_limit_bytes=VMEM_LIMIT_BYTES),
        name="merge",
    )(x2d, y_ret, y_att, y_sgu, gates, wb_bf, wo_bf, ln_g, ln_b)


PEER_PAD_ID = 1024.0


def _peer_candidate_ids():
    pairs = [(0, b) for b in range(PEER_TOPK)]
    pairs += [(a, b) for a in range(1, SUBLANES) for b in range(SUBLANES)]
    pairs += [(a, 0) for a in range(SUBLANES, PEER_TOPK)]
    ids = [float(a * PEER_TOPK + b) if (a + 1) * (b + 1) <= PEER_TOPK else PEER_PAD_ID for a, b in pairs]
    return np.asarray(ids, np.float32)


def _topk_rows(v, ids, k, group):
    nblk = v.shape[0] // SUBLANES
    lists = []
    for g0 in range(0, nblk, group):
        lv = [v[b * SUBLANES:(b + 1) * SUBLANES] for b in range(g0, g0 + group)]
        li = [ids[b * SUBLANES:(b + 1) * SUBLANES] for b in range(g0, g0 + group)]
        for rnd in range(group):
            for a in range(rnd % 2, group - 1, 2):
                swap = lv[a + 1] > lv[a]
                lv[a], lv[a + 1] = jnp.where(swap, lv[a + 1], lv[a]), jnp.where(swap, lv[a], lv[a + 1])
                li[a], li[a + 1] = jnp.where(swap, li[a + 1], li[a]), jnp.where(swap, li[a], li[a + 1])
        lists.append((lv, li))
    vals, idxs = [], []
    for _ in range(k):
        head = functools.reduce(jnp.maximum, [lv[0] for lv, _ in lists])
        m = jnp.max(head, axis=0, keepdims=True)
        cand = functools.reduce(jnp.minimum,
                                [jnp.where(lv[0] == m, li[0], 2.0 * PEER_PAD_ID) for lv, li in lists])
        idx = jnp.min(cand, axis=0, keepdims=True)
        vals.append(m)
        idxs.append(idx)
        for lv, li in lists:
            won = li[0] == idx
            for j in range(group - 1):
                lv[j] = jnp.where(won, lv[j + 1], lv[j])
                li[j] = jnp.where(won, li[j + 1], li[j])
            lv[group - 1] = jnp.where(won, NEG_INF, lv[group - 1])
    return vals, idxs


def _route_kernel(x_ref, wq_ref, keys_ref, cid_ref, i1_ref, i2_ref, gate_ref):
    tb = x_ref.shape[0]
    q = jnp.dot(x_ref[...].astype(BF16), wq_ref[...], preferred_element_type=F32).astype(BF16)
    half = PEER_DQ // 2
    key_row = lax.broadcasted_iota(jnp.int32, (PEER_NKEYS, tb), 0).astype(F32)
    arow = lax.broadcasted_iota(jnp.int32, (PEER_TOPK, tb), 0).astype(F32)
    cand_ids = cid_ref[...]
    i1_rows, i2_rows, gate_rows = [], [], []
    for h in range(PEER_HEADS):
        sub = []
        for p in range(2):
            qs = q[:, (h * 2 + p) * half:(h * 2 + p + 1) * half]
            sc = lax.dot_general(keys_ref[h * 2 + p], qs, (((1,), (1,)), ((), ())),
                                 preferred_element_type=F32)
            sub.append(_topk_rows(sc, key_row, PEER_TOPK, group=8))
        (s1, i1), (s2, i2) = sub
        s1m = jnp.concatenate(s1, axis=0)
        s2m = jnp.concatenate(s2, axis=0)
        i1m = jnp.concatenate(i1, axis=0)
        i2m = jnp.concatenate(i2, axis=0)
        pieces = [s1[0] + s2m[:SUBLANES], s1[0] + s2m[SUBLANES:]]
        pieces += [s1[a] + s2m[:SUBLANES] for a in range(1, SUBLANES)]
        pieces += [s1m[SUBLANES:] + s2[0]]
        cand = jnp.where(cand_ids < PEER_PAD_ID, jnp.concatenate(pieces, axis=0), NEG_INF)
        top_s, top_pos = _topk_rows(cand, cand_ids, PEER_TOPK, group=5)
        top_s = jnp.concatenate(top_s, axis=0)
        e = jnp.exp(top_s - top_s[0:1, :])
        gate_rows.append(e / jnp.sum(e, axis=0, keepdims=True))
        for pos in top_pos:
            a = jnp.floor(pos * (1.0 / PEER_TOPK))
            b = pos - a * PEER_TOPK
            i1_rows.append(jnp.sum(jnp.where(arow == a, i1m, 0.0), axis=0, keepdims=True))
            i2_rows.append(jnp.sum(jnp.where(arow == b, i2m, 0.0), axis=0, keepdims=True))
    i1_ref[...] = jnp.concatenate(i1_rows, axis=0).T
    i2_ref[...] = jnp.concatenate(i2_rows, axis=0).T
    gate_ref[...] = jnp.concatenate(gate_rows, axis=0).T


def _route(x1, wq_bf, keys_bf, *, tb=128):
    t = x1.shape[0]
    hk = PEER_HEADS * PEER_TOPK
    row = lambda i: (i, 0)
    ids = _peer_candidate_ids()
    cand_ids = jnp.asarray(np.ascontiguousarray(np.broadcast_to(ids[:, None], (ids.shape[0], tb))))
    return pl.pallas_call(
        _route_kernel,
        grid=(t // tb,),
        in_specs=[
            pl.BlockSpec((tb, D_MODEL), row),
            pl.BlockSpec((D_MODEL, PEER_HEADS * PEER_DQ), lambda i: (0, 0)),
            pl.BlockSpec((PEER_HEADS * 2, PEER_NKEYS, PEER_DQ // 2), lambda i: (0, 0, 0)),
            pl.BlockSpec(cand_ids.shape, lambda i: (0, 0)),
        ],
        out_specs=[pl.BlockSpec((tb, hk), row)] * 3,
        out_shape=[jax.ShapeDtypeStruct((t, hk), F32)] * 3,
        compiler_params=pltpu.CompilerParams(dimension_semantics=("arbitrary",),
                                             vmem_limit_bytes=VMEM_LIMIT_BYTES),
        name="peer_route",
    )(x1, wq_bf, keys_bf, cand_ids)


def _fp8_scale(bound):
    return jnp.exp2(jnp.floor(jnp.log2(FP8_TARGET / jnp.maximum(bound, FP8_TINY))))


def _peer_kernel(x_ref, i1_ref, i2_ref, gate_ref, ut_ref, stat_ref, v_ref, g_ref, b_ref, o_ref,
                 gs_ref, x8_ref, hscale_ref, wscale_ref, acc_ref, *, tb, ec, pitch, unroll):
    e = pl.program_id(1)
    ncl = ec // PEER_NKEYS

    @pl.when(e == 0)
    def _():
        x = x_ref[...]
        sx = _fp8_scale(jnp.max(jnp.max(jnp.abs(x), axis=0, keepdims=True), axis=1, keepdims=True))
        x8_ref[...] = (x * sx).astype(FP8)
        hscale_ref[...] = jnp.broadcast_to(stat_ref[0:1, 0:1] / sx, hscale_ref.shape)
        row_norm = jnp.sqrt(jnp.sum(x * x, axis=1, keepdims=True))
        sw = _fp8_scale(PEER_HEADS * row_norm * stat_ref[0:1, 1:2])
        wscale_ref[...] = jnp.broadcast_to(sw, wscale_ref.shape)
        acc_ref[...] = jnp.zeros_like(acc_ref)
        key_id = lax.broadcasted_iota(jnp.int32, (PEER_NKEYS, PEER_NKEYS), 0).astype(F32)

        def build(blk, carry):
            for r in range(unroll):
                t = blk * unroll + r
                i1 = jnp.broadcast_to(i1_ref[pl.ds(t, 1), :], (PEER_NKEYS, PEER_NKEYS))
                i2 = jnp.broadcast_to(i2_ref[pl.ds(t, 1), :], (PEER_NKEYS, PEER_NKEYS))
                gt = jnp.broadcast_to(0.5 * gate_ref[pl.ds(t, 1), :] * wscale_ref[pl.ds(t, 1), :]
                                      * hscale_ref[0:1, :], (PEER_NKEYS, PEER_NKEYS))
                a_t = jnp.where(i1 == key_id, gt, 0.0).astype(BF16)
                b_t = jnp.where(i2 == key_id, 1.0, 0.0).astype(BF16)
                g_tok = lax.dot_general(a_t, b_t, (((1,), (1,)), ((), ())), preferred_element_type=F32)
                g_rnd = g_tok.astype(BF16).astype(F32).reshape(PEER_NKEYS // PEER_PAIR, PEER_PAIR, PEER_NKEYS)
                hi = lax.bitcast_convert_type(g_rnd[:, :SUBLANES, :].reshape(PEER_PLANES, PEER_NKEYS), jnp.uint32)
                lo = lax.bitcast_convert_type(g_rnd[:, SUBLANES:, :].reshape(PEER_PLANES, PEER_NKEYS),
                                              jnp.uint32) >> 16
                gs_ref[pl.ds(t, PEER_PLANES, stride=pitch), :] = hi | lo
            return carry

        lax.fori_loop(0, tb // unroll, build, 0)

    d = jnp.dot(x8_ref[...], ut_ref[...], preferred_element_type=F32)
    act = d * (1.0 + lax.erf(d * (hscale_ref[0:1, 0:1] * 2.0 ** -0.5)))
    parts = [None] * ncl
    for grp in range(ncl // PEER_PAIR):
        for i in range(SUBLANES):
            plane = (e * (ncl // PEER_PAIR) + grp) * SUBLANES + i
            word = gs_ref[pl.ds(pl.multiple_of(plane * pitch, SUBLANES), tb), :]
            halves = (lax.bitcast_convert_type(word & jnp.uint32(0xFFFF0000), F32),
                      lax.bitcast_convert_type(word << 16, F32))
            for half, gval in enumerate(halves):
                cl = grp * PEER_PAIR + half * SUBLANES + i
                parts[cl] = act[:, cl * PEER_NKEYS:(cl + 1) * PEER_NKEYS] * gval
    w = jnp.concatenate(parts, axis=1).astype(FP8)
    acc_ref[...] += jnp.dot(w, v_ref[...], preferred_element_type=F32)

    @pl.when(e == pl.num_programs(1) - 1)
    def _():
        peer = acc_ref[...] * (stat_ref[0:1, 2:3] / wscale_ref[:, 0:1])
        o_ref[...] = _layer_norm_rows(DEEPNORM_ALPHA * x_ref[...] + peer, g_ref[...], b_ref[...])


def _peer(x1, i1, i2, gate, ut_f8, stats, v_f8, ln_g, ln_b, *, tb=512, ec=2048, unroll=32):
    t = x1.shape[0]
    n_exp = v_f8.shape[0]
    hk = PEER_HEADS * PEER_TOPK
    pitch = tb + SUBLANES
    row = lambda i, e: (i, 0)
    const = lambda i, e: (0, 0)
    return pl.pallas_call(
        functools.partial(_peer_kernel, tb=tb, ec=ec, pitch=pitch, unroll=unroll),
        grid=(t // tb, n_exp // ec),
        in_specs=[
            pl.BlockSpec((tb, D_MODEL), row),
            pl.BlockSpec((tb, hk), row),
            pl.BlockSpec((tb, hk), row),
            pl.BlockSpec((tb, hk), row),
            pl.BlockSpec((D_MODEL, ec), lambda i, e: (0, e)),
            pl.BlockSpec((1, LANES), const),
            pl.BlockSpec((ec, D_MODEL), lambda i, e: (e, 0)),
            pl.BlockSpec((1, D_MODEL), const),
            pl.BlockSpec((1, D_MODEL), const),
        ],
        out_specs=pl.BlockSpec((tb, D_MODEL), row),
        out_shape=jax.ShapeDtypeStruct((t, D_MODEL), F32),
        scratch_shapes=[
            pltpu.VMEM((PEER_PLANES * pitch, PEER_NKEYS), jnp.uint32),
            pltpu.VMEM((tb, D_MODEL), FP8),
            pltpu.VMEM((SUBLANES, LANES), F32),
            pltpu.VMEM((tb, LANES), F32),
            pltpu.VMEM((tb, D_MODEL), F32),
        ],
        compiler_params=pltpu.CompilerParams(dimension_semantics=("arbitrary", "arbitrary"),
                                             vmem_limit_bytes=VMEM_LIMIT_BYTES),
        name="peer_dense",
    )(x1, i1, i2, gate, ut_f8, stats, v_f8, ln_g, ln_b)


def _rope_tables(seq):
    rows = seq // GRID_W
    row = jnp.repeat(jnp.arange(rows, dtype=F32), GRID_W)
    col = jnp.tile(jnp.arange(GRID_W, dtype=F32), rows)
    n_freq = ATT_DH // 4
    inv_freq = ROPE_THETA ** (-jnp.arange(n_freq, dtype=F32) / n_freq)
    ang = jnp.concatenate([row[:, None] * inv_freq, col[:, None] * inv_freq], axis=-1)
    cos, sin = jnp.cos(ang), jnp.sin(ang)
    return jnp.concatenate([cos, cos], axis=-1), jnp.concatenate([-sin, sin], axis=-1)


def _layer(x2d, batch, seq, cos2, sin2, p):
    rq, rk, rv, rg, aq, ak, av, y_sgu, gates = _inproj(
        x2d, seq, p["w_in"], cos2, sin2, p["b_gate"], p["qn"], p["kn"], p["sgu_ln_g"], p["sgu_ln_b"],
        p["sgu_w"], p["sgu_bt"])
    y_ret = _retention(rq, rk, rv, rg, p["lg"], batch, seq)
    y_att = _attention(aq, ak, jnp.transpose(av), batch, seq)
    x1 = _merge(x2d, y_ret, y_att, y_sgu, gates, p["w_branch"], p["w_out"], p["ln1_g"], p["ln1_b"])
    i1, i2, gate = _route(x1, p["peer_wq"], p["peer_keys"])
    return _peer(x1, i1, i2, gate, p["peer_ut"], p["peer_stats"], p["peer_v"], p["ln2_g"], p["ln2_b"])


def _prep_layer(l, w_in, b_gate, ret_decay_fwd, ret_decay_bwd, attn_q_norm, attn_k_norm,
                sgu_ln_g, sgu_ln_b, sgu_w, sgu_b, w_branch, w_out, ln1_g, ln1_b,
                peer_wq, peer_keys, peer_u, peer_v, ln2_g, ln2_b):
    half = PEER_DQ // 2
    u_scale = _fp8_scale(jnp.max(jnp.abs(peer_u[l])))
    v_scale = _fp8_scale(jnp.max(jnp.abs(peer_v[l])))
    u_norm = jnp.sqrt(jnp.max(jnp.sum(jnp.square(peer_u[l]), axis=1)))
    stats = jnp.zeros((1, LANES), F32).at[0, 0].set(1.0 / u_scale).at[0, 1].set(u_norm).at[0, 2].set(1.0 / v_scale)
    return dict(
        w_in=w_in[l].astype(BF16),
        b_gate=b_gate[l].reshape(1, GATE_W),
        lg=jnp.stack([jax.nn.log_sigmoid(ret_decay_fwd[l].astype(F32)),
                      jax.nn.log_sigmoid(ret_decay_bwd[l].astype(F32))]),
        qn=attn_q_norm[l].reshape(1, ATT_DH),
        kn=attn_k_norm[l].reshape(1, ATT_DH),
        sgu_ln_g=sgu_ln_g[l].reshape(1, SGU_WIDTH),
        sgu_ln_b=sgu_ln_b[l].reshape(1, SGU_WIDTH),
        sgu_w=sgu_w[l].astype(BF16),
        sgu_bt=jnp.transpose(sgu_b[l]),
        w_branch=w_branch[l].astype(BF16),
        w_out=w_out[l].astype(BF16),
        ln1_g=ln1_g[l].reshape(1, D_MODEL),
        ln1_b=ln1_b[l].reshape(1, D_MODEL),
        peer_wq=peer_wq[l].astype(BF16),
        peer_keys=peer_keys[l].reshape(PEER_HEADS * 2, PEER_NKEYS, half).astype(BF16),
        peer_ut=jnp.transpose(peer_u[l] * u_scale).astype(FP8),
        peer_stats=stats,
        peer_v=(peer_v[l] * v_scale).astype(FP8),
        ln2_g=ln2_g[l].reshape(1, D_MODEL),
        ln2_b=ln2_b[l].reshape(1, D_MODEL),
    )


def kernel(x_prompt, x_sample, w_in, b_gate, ret_decay_fwd, ret_decay_bwd, attn_q_norm, attn_k_norm,
           sgu_ln_g, sgu_ln_b, sgu_w, sgu_b, w_branch, w_out, ln1_g, ln1_b,
           peer_wq, peer_keys, peer_u, peer_v, ln2_g, ln2_b):
    weights = (w_in, b_gate, ret_decay_fwd, ret_decay_bwd, attn_q_norm, attn_k_norm,
               sgu_ln_g, sgu_ln_b, sgu_w, sgu_b, w_branch, w_out, ln1_g, ln1_b,
               peer_wq, peer_keys, peer_u, peer_v, ln2_g, ln2_b)
    layers = [_prep_layer(l, *weights) for l in range(DEPTH)]
    outs = []
    for x in (x_prompt, x_sample):
        batch, seq, _ = x.shape
        cos2, sin2 = _rope_tables(seq)
        h = x.reshape(batch * seq, D_MODEL)
        for p in layers:
            h = _layer(h, batch, seq, cos2, sin2, p)
        outs.append(h.reshape(batch, seq, D_MODEL))
    return tuple(outs)
```

```python
import functools
import math

import jax
import jax.numpy as jnp
import numpy as np
from jax import lax
from jax.experimental import pallas as pl
from jax.experimental.pallas import tpu as pltpu

F32 = jnp.float32
BF16 = jnp.bfloat16
FP8 = jnp.float8_e4m3fn

D_MODEL = 1024
DEPTH = 2
GRID_W = 64
CHUNK = 128
RET_HEADS = 4
RET_DK = 128
RET_DV = 256
ATT_HEADS = 8
ATT_KV_HEADS = 2
ATT_DH = 128
ATT_GROUP = ATT_HEADS // ATT_KV_HEADS
SGU_GROUPS = 4
SGU_WIDTH = 1024
SGU_GW = SGU_WIDTH // SGU_GROUPS
PEER_HEADS = 8
PEER_NKEYS = 128
PEER_DQ = 256
PEER_TOPK = 16
PEER_PLANES = PEER_NKEYS // 2
PEER_PAIR = 16
N_BRANCH = 3
ROPE_THETA = 10000.0
DEEPNORM_ALPHA = (2 * DEPTH) ** 0.25
LN_EPS = 1e-5
RMS_EPS = 1e-6

RET_QK_W = RET_HEADS * RET_DK
RET_V_W = RET_HEADS * RET_DV
ATT_Q_W = ATT_HEADS * ATT_DH
ATT_KV_W = ATT_KV_HEADS * ATT_DH
GATE_W = N_BRANCH * D_MODEL
SPLITS = (RET_QK_W, RET_QK_W, RET_V_W, RET_V_W, ATT_Q_W, ATT_KV_W, ATT_KV_W, SGU_WIDTH, SGU_WIDTH, GATE_W)
OFFS = tuple(int(sum(SPLITS[:i])) for i in range(len(SPLITS) + 1))
IN_W = OFFS[-1]

LANES = 128
SUBLANES = 8
VMEM_LIMIT_BYTES = 56 * 1024 * 1024

NEG_INF = float("-inf")
FP8_TARGET = 224.0
FP8_TINY = 1e-30
ATT_Q_SCALE = ATT_DH ** -0.5 * math.log2(math.e)
ATT_LOGIT_LIMIT = 64.0
ATT_BOUND_SLACK = 1.02


def _gelu(x):
    return 0.5 * x * (1.0 + lax.erf(x * (2.0 ** -0.5)))


def _rope(x, cos2, sin2):
    return x * cos2 + pltpu.roll(x, ATT_DH // 2, 1) * sin2


def _layer_norm_rows(x, g, b):
    mu = jnp.mean(x, axis=-1, keepdims=True)
    xc = x - mu
    var = jnp.mean(xc * xc, axis=-1, keepdims=True)
    return xc * lax.rsqrt(var + LN_EPS) * g + b


def _inproj_kernel(x_ref, w_ref, cos_ref, sin_ref, bgate_ref, qn_ref, kn_ref, lng_ref, lnb_ref,
                   sw_ref, sb_ref,
                   rq_ref, rk_ref, rv_ref, rg_ref, aq_ref, ak_ref, av_ref, ysgu_ref, gates_ref):
    tb = x_ref.shape[0]
    xb = x_ref[...].astype(BF16)
    cos2 = cos_ref[...]
    sin2 = sin_ref[...]

    def proj(i):
        return jnp.dot(xb, w_ref[:, OFFS[i]:OFFS[i + 1]], preferred_element_type=F32)

    h = proj(0)
    for hd in range(RET_HEADS):
        sl = slice(hd * RET_DK, (hd + 1) * RET_DK)
        rq_ref[:, sl] = _rope(h[:, sl], cos2, sin2).astype(BF16)
    h = proj(1)
    for hd in range(RET_HEADS):
        sl = slice(hd * RET_DK, (hd + 1) * RET_DK)
        rk_ref[:, sl] = (_rope(h[:, sl], cos2, sin2) * (RET_DK ** -0.5)).astype(BF16)
    rv_ref[...] = proj(2).astype(BF16)
    h = proj(3)
    rg_ref[...] = (h * jax.nn.sigmoid(h)).astype(BF16)

    h = proj(4)
    qn = qn_ref[...]
    for hd in range(ATT_HEADS):
        sl = slice(hd * ATT_DH, (hd + 1) * ATT_DH)
        t = h[:, sl]
        t = t * lax.rsqrt(jnp.mean(t * t, axis=-1, keepdims=True) + RMS_EPS) * qn
        aq_ref[:, sl] = (_rope(t, cos2, sin2) * ATT_Q_SCALE).astype(BF16)
    h = proj(5)
    kn = kn_ref[...]
    for hd in range(ATT_KV_HEADS):
        sl = slice(hd * ATT_DH, (hd + 1) * ATT_DH)
        t = h[:, sl]
        t = t * lax.rsqrt(jnp.mean(t * t, axis=-1, keepdims=True) + RMS_EPS) * kn
        ak_ref[:, sl] = _rope(t, cos2, sin2).astype(BF16)
    av_ref[...] = proj(6).astype(BF16)

    u = _gelu(proj(7))
    vn = _layer_norm_rows(_gelu(proj(8)), lng_ref[...], lnb_ref[...]).astype(BF16)
    sb = sb_ref[...]
    for c in range(tb // CHUNK):
        rows = slice(c * CHUNK, (c + 1) * CHUNK)
        for g in range(SGU_GROUPS):
            cols = slice(g * SGU_GW, (g + 1) * SGU_GW)
            mixed = jnp.dot(sw_ref[g], vn[rows, cols], preferred_element_type=F32) + sb[:, g:g + 1]
            ysgu_ref[rows, cols] = (u[rows, cols] * mixed).astype(BF16)

    gates_ref[...] = jax.nn.sigmoid(proj(9) + bgate_ref[...]).astype(BF16)


def _inproj(x2d, seq, w_bf, cos2, sin2, b_gate, qn, kn, lng, lnb, sgu_w_bf, sgu_bt, *, tb=256):
    t = x2d.shape[0]
    nblk_seq = seq // tb
    const = lambda i: (0, 0)
    row = lambda i: (i, 0)
    pos = lambda i: (i % nblk_seq, 0)
    outs = [(RET_QK_W, BF16), (RET_QK_W, BF16), (RET_V_W, BF16), (RET_V_W, BF16), (ATT_Q_W, BF16),
            (ATT_KV_W, BF16), (ATT_KV_W, BF16), (SGU_WIDTH, BF16), (GATE_W, BF16)]
    return pl.pallas_call(
        _inproj_kernel,
        grid=(t // tb,),
        in_specs=[
            pl.BlockSpec((tb, D_MODEL), row),
            pl.BlockSpec((D_MODEL, IN_W), const, pipeline_mode=pl.Buffered(1)),
            pl.BlockSpec((tb, ATT_DH), pos),
            pl.BlockSpec((tb, ATT_DH), pos),
            pl.BlockSpec((1, GATE_W), const),
            pl.BlockSpec((1, ATT_DH), const),
            pl.BlockSpec((1, ATT_DH), const),
            pl.BlockSpec((1, SGU_WIDTH), const),
            pl.BlockSpec((1, SGU_WIDTH), const),
            pl.BlockSpec((SGU_GROUPS, CHUNK, CHUNK), lambda i: (0, 0, 0)),
            pl.BlockSpec((CHUNK, SGU_GROUPS), const),
        ],
        out_specs=[pl.BlockSpec((tb, w), row) for w, _ in outs],
        out_shape=[jax.ShapeDtypeStruct((t, w), d) for w, d in outs],
        compiler_params=pltpu.CompilerParams(dimension_semantics=("arbitrary",),
                                             vmem_limit_bytes=VMEM_LIMIT_BYTES),
        name="inproj",
    )(x2d, w_bf, cos2, sin2, b_gate, qn, kn, lng, lnb, sgu_w_bf, sgu_bt)


def _retention_kernel(lg_ref, q_ref, k_ref, v_ref, *rest, rb, forward):
    if forward:
        g_ref, yb_ref, o_ref, state_ref = rest
    else:
        o_ref, state_ref = rest
    nchunk = rb // CHUNK

    @pl.when(pl.program_id(1) == 0)
    def _():
        state_ref[...] = jnp.zeros_like(state_ref)

    n_i = lax.broadcasted_iota(jnp.int32, (CHUNK, CHUNK), 0)
    m_i = lax.broadcasted_iota(jnp.int32, (CHUNK, CHUNK), 1)
    col = lax.broadcasted_iota(jnp.int32, (CHUNK, 1), 0).astype(F32)
    dist = ((n_i - m_i) if forward else (m_i - n_i)).astype(F32)
    live = (n_i >= m_i) if forward else (m_i > n_i)

    for hd in range(RET_HEADS):
        lg = lg_ref[0 if forward else 1, hd]
        decay = jnp.where(live, jnp.exp(lg * jnp.maximum(dist, 0.0)), 0.0)
        zeta = jnp.exp(lg * ((CHUNK - 1.0 - col) if forward else col))
        xi = jnp.exp(lg * ((col + 1.0) if forward else (CHUNK - col)))
        chunk_decay = jnp.exp(lg * CHUNK)
        qk_cols = slice(hd * RET_DK, (hd + 1) * RET_DK)
        v_cols = slice(hd * RET_DV, (hd + 1) * RET_DV)
        for c in (range(nchunk) if forward else reversed(range(nchunk))):
            rows = slice(c * CHUNK, (c + 1) * CHUNK)
            q = q_ref[rows, qk_cols]
            k = k_ref[rows, qk_cols]
            v = v_ref[rows, v_cols]
            sc = lax.dot_general(q, k, (((1,), (1,)), ((), ())), preferred_element_type=F32) * decay
            y = jnp.dot(sc.astype(BF16), v, preferred_element_type=F32)
            st = state_ref[hd]
            y = y + jnp.dot(q, st.astype(BF16), preferred_element_type=F32) * xi
            vz = (v.astype(F32) * zeta).astype(BF16)
            upd = lax.dot_general(k, vz, (((0,), (0,)), ((), ())), preferred_element_type=F32)
            state_ref[hd] = st * chunk_decay + upd
            if forward:
                y = y + yb_ref[rows, v_cols]
                mu = jnp.mean(y, axis=-1, keepdims=True)
                yc = y - mu
                var = jnp.mean(yc * yc, axis=-1, keepdims=True)
                yn = yc * lax.rsqrt(var + LN_EPS)
                o_ref[rows, v_cols] = (g_ref[rows, v_cols].astype(F32) * yn).astype(BF16)
            else:
                o_ref[rows, v_cols] = y


def _retention_pass(lg, rq, rk, rv, extra, batch, seq, *, rb, forward):
    t = rq.shape[0]
    ns = seq // rb
    if forward:
        blk = lambda b, s, lg_ref: (b * ns + s, 0)
    else:
        blk = lambda b, s, lg_ref: (b * ns + ns - 1 - s, 0)
    grid_spec = pltpu.PrefetchScalarGridSpec(
        num_scalar_prefetch=1,
        grid=(batch, ns),
        in_specs=[pl.BlockSpec((rb, RET_QK_W), blk), pl.BlockSpec((rb, RET_QK_W), blk),
                  pl.BlockSpec((rb, RET_V_W), blk)] + [pl.BlockSpec((rb, RET_V_W), blk)] * len(extra),
        out_specs=pl.BlockSpec((rb, RET_V_W), blk),
        scratch_shapes=[pltpu.VMEM((RET_HEADS, RET_DK, RET_DV), F32)],
    )
    return pl.pallas_call(
        functools.partial(_retention_kernel, rb=rb, forward=forward),
        grid_spec=grid_spec,
        out_shape=jax.ShapeDtypeStruct((t, RET_V_W), BF16 if forward else F32),
        compiler_params=pltpu.CompilerParams(dimension_semantics=("arbitrary", "arbitrary"),
                                             vmem_limit_bytes=VMEM_LIMIT_BYTES),
        name="retention_fwd" if forward else "retention_bwd",
    )(lg, rq, rk, rv, *extra)


def _retention(rq, rk, rv, rg, lg, batch, seq, *, rb=512):
    y_bwd = _retention_pass(lg, rq, rk, rv, (), batch, seq, rb=rb, forward=False)
    return _retention_pass(lg, rq, rk, rv, (rg, y_bwd), batch, seq, rb=rb, forward=True)


def _attention_kernel(q_ref, k_ref, vt_ref, o_ref, m_ref, l_ref, acc_ref, *, bounded):
    ki = pl.program_id(3)

    @pl.when(ki == 0)
    def _():
        if not bounded:
            m_ref[...] = jnp.full_like(m_ref, NEG_INF)
        l_ref[...] = jnp.zeros_like(l_ref)
        acc_ref[...] = jnp.zeros_like(acc_ref)

    k = k_ref[...]
    vt = vt_ref[...]
    def scores(g):
        q = q_ref[:, g * ATT_DH:(g + 1) * ATT_DH]
        return lax.dot_general(k, q, (((1,), (1,)), ((), ())), preferred_element_type=F32)

    st_next = scores(0)
    for g in range(ATT_GROUP):
        st = st_next
        if g + 1 < ATT_GROUP:
            st_next = scores(g + 1)
        if bounded:
            pt = jnp.exp2(st)
            l_ref[g:g + 1, :] += jnp.sum(pt, axis=0, keepdims=True)
            acc_ref[g] += jnp.dot(vt, pt.astype(BF16), preferred_element_type=F32)
            continue
        m_prev = m_ref[g:g + 1, :]
        m_new = jnp.maximum(m_prev, jnp.max(st, axis=0, keepdims=True))
        alpha = jnp.exp2(m_prev - m_new)
        pt = jnp.exp2(st - m_new)
        l_ref[g:g + 1, :] = alpha * l_ref[g:g + 1, :] + jnp.sum(pt, axis=0, keepdims=True)
        acc_ref[g] = alpha * acc_ref[g] + jnp.dot(vt, pt.astype(BF16), preferred_element_type=F32)
        m_ref[g:g + 1, :] = m_new

    @pl.when(ki == pl.num_programs(3) - 1)
    def _():
        for g in range(ATT_GROUP):
            out = acc_ref[g] / l_ref[g:g + 1, :]
            o_ref[:, g * ATT_DH:(g + 1) * ATT_DH] = out.T.astype(BF16)


def _attention(aq, ak, avt, score_bound, batch, seq):
    return lax.cond(score_bound <= ATT_LOGIT_LIMIT,
                    functools.partial(_attention_call, batch=batch, seq=seq, bounded=True),
                    functools.partial(_attention_call, batch=batch, seq=seq, bounded=False),
                    aq, ak, avt)


def _attention_call(aq, ak, avt, *, batch, seq, bounded, tq=512, tk=4096):
    t = aq.shape[0]
    nq = seq // tq
    nk = seq // tk
    gw = ATT_GROUP * ATT_DH
    return pl.pallas_call(
        functools.partial(_attention_kernel, bounded=bounded),
        grid=(batch, ATT_KV_HEADS, nq, nk),
        in_specs=[
            pl.BlockSpec((tq, gw), lambda b, g, qi, ki: (b * nq + qi, g)),
            pl.BlockSpec((tk, ATT_DH), lambda b, g, qi, ki: (b * nk + ki, g)),
            pl.BlockSpec((ATT_DH, tk), lambda b, g, qi, ki: (g, b * nk + ki)),
        ],
        out_specs=pl.BlockSpec((tq, gw), lambda b, g, qi, ki: (b * nq + qi, g)),
        out_shape=jax.ShapeDtypeStruct((t, ATT_Q_W), BF16),
        scratch_shapes=[
            pltpu.VMEM((SUBLANES, tq), F32),
            pltpu.VMEM((SUBLANES, tq), F32),
            pltpu.VMEM((ATT_GROUP, ATT_DH, tq), F32),
        ],
        compiler_params=pltpu.CompilerParams(
            dimension_semantics=("arbitrary", "arbitrary", "arbitrary", "arbitrary"),
            vmem_limit_bytes=VMEM_LIMIT_BYTES),
        name="attention_bounded" if bounded else "attention",
    )(aq, ak, avt)


def _merge_kernel(x_ref, yr_ref, ya_ref, ys_ref, gates_ref, wb_ref, wo_ref, g_ref, b_ref, o_ref):
    merged = None
    for i, y_ref in enumerate((yr_ref, ya_ref, ys_ref)):
        gate = gates_ref[:, i * D_MODEL:(i + 1) * D_MODEL].astype(F32)
        term = gate * jnp.dot(y_ref[...], wb_ref[i], preferred_element_type=F32)
        merged = term if merged is None else merged + term
    proj = jnp.dot(merged.astype(BF16), wo_ref[...], preferred_element_type=F32)
    o_ref[...] = _layer_norm_rows(DEEPNORM_ALPHA * x_ref[...] + proj, g_ref[...], b_ref[...])


def _merge(x2d, y_ret, y_att, y_sgu, gates, wb_bf, wo_bf, ln_g, ln_b, *, tb=512):
    t = x2d.shape[0]
    row = lambda i: (i, 0)
    const = lambda i: (0, 0)
    return pl.pallas_call(
        _merge_kernel,
        grid=(t // tb,),
        in_specs=[
            pl.BlockSpec((tb, D_MODEL), row),
            pl.BlockSpec((tb, D_MODEL), row),
            pl.BlockSpec((tb, D_MODEL), row),
            pl.BlockSpec((tb, D_MODEL), row),
            pl.BlockSpec((tb, GATE_W), row),
            pl.BlockSpec((N_BRANCH, D_MODEL, D_MODEL), lambda i: (0, 0, 0)),
            pl.BlockSpec((D_MODEL, D_MODEL), const),
            pl.BlockSpec((1, D_MODEL), const),
            pl.BlockSpec((1, D_MODEL), const),
        ],
        out_specs=pl.BlockSpec((tb, D_MODEL), row),
        out_shape=jax.ShapeDtypeStruct((t, D_MODEL), F32),
        compiler_params=pltpu.CompilerParams(dimension_semantics=("arbitrary",),
                                             vmem_limit_bytes=VMEM_LIMIT_BYTES),
        name="merge",
    )(x2d, y_ret, y_att, y_sgu, gates, wb_bf, wo_bf, ln_g, ln_b)


PEER_PAD_ID = 1024.0


def _peer_candidate_ids():
    pairs = [(0, b) for b in range(PEER_TOPK)]
    pairs += [(a, b) for a in range(1, SUBLANES) for b in range(SUBLANES)]
    pairs += [(a, 0) for a in range(SUBLANES, PEER_TOPK)]
    ids = [float(a * PEER_TOPK + b) if (a + 1) * (b + 1) <= PEER_TOPK else PEER_PAD_ID for a, b in pairs]
    return np.asarray(ids, np.float32)


def _topk_rows(v, ids, k, group):
    nblk = v.shape[0] // SUBLANES
    lists = []
    for g0 in range(0, nblk, group):
        lv = [v[b * SUBLANES:(b + 1) * SUBLANES] for b in range(g0, g0 + group)]
        li = [ids[b * SUBLANES:(b + 1) * SUBLANES] for b in range(g0, g0 + group)]
        for rnd in range(group):
            for a in range(rnd % 2, group - 1, 2):
                swap = lv[a + 1] > lv[a]
                lv[a], lv[a + 1] = jnp.where(swap, lv[a + 1], lv[a]), jnp.where(swap, lv[a], lv[a + 1])
                li[a], li[a + 1] = jnp.where(swap, li[a + 1], li[a]), jnp.where(swap, li[a], li[a + 1])
        lists.append((lv, li))
    vals, idxs = [], []
    for _ in range(k):
        head = functools.reduce(jnp.maximum, [lv[0] for lv, _ in lists])
        m = jnp.max(head, axis=0, keepdims=True)
        cand = functools.reduce(jnp.minimum,
                                [jnp.where(lv[0] == m, li[0], 2.0 * PEER_PAD_ID) for lv, li in lists])
        idx = jnp.min(cand, axis=0, keepdims=True)
        vals.append(m)
        idxs.append(idx)
        for lv, li in lists:
            won = li[0] == idx
            for j in range(group - 1):
                lv[j] = jnp.where(won, lv[j + 1], lv[j])
                li[j] = jnp.where(won, li[j + 1], li[j])
            lv[group - 1] = jnp.where(won, NEG_INF, lv[group - 1])
    return vals, idxs


def _route_kernel(x_ref, wq_ref, keys_ref, cid_ref, i1_ref, i2_ref, gate_ref):
    tb = x_ref.shape[0]
    q = jnp.dot(x_ref[...].astype(BF16), wq_ref[...], preferred_element_type=F32).astype(BF16)
    half = PEER_DQ // 2
    key_row = lax.broadcasted_iota(jnp.int32, (PEER_NKEYS, tb), 0).astype(F32)
    arow = lax.broadcasted_iota(jnp.int32, (PEER_TOPK, tb), 0).astype(F32)
    cand_ids = cid_ref[...]
    i1_rows, i2_rows, gate_rows = [], [], []
    for h in range(PEER_HEADS):
        sub = []
        for p in range(2):
            qs = q[:, (h * 2 + p) * half:(h * 2 + p + 1) * half]
            sc = lax.dot_general(keys_ref[h * 2 + p], qs, (((1,), (1,)), ((), ())),
                                 preferred_element_type=F32)
            sub.append(_topk_rows(sc, key_row, PEER_TOPK, group=8))
        (s1, i1), (s2, i2) = sub
        s1m = jnp.concatenate(s1, axis=0)
        s2m = jnp.concatenate(s2, axis=0)
        i1m = jnp.concatenate(i1, axis=0)
        i2m = jnp.concatenate(i2, axis=0)
        pieces = [s1[0] + s2m[:SUBLANES], s1[0] + s2m[SUBLANES:]]
        pieces += [s1[a] + s2m[:SUBLANES] for a in range(1, SUBLANES)]
        pieces += [s1m[SUBLANES:] + s2[0]]
        cand = jnp.where(cand_ids < PEER_PAD_ID, jnp.concatenate(pieces, axis=0), NEG_INF)
        top_s, top_pos = _topk_rows(cand, cand_ids, PEER_TOPK, group=5)
        top_s = jnp.concatenate(top_s, axis=0)
        e = jnp.exp(top_s - top_s[0:1, :])
        gate_rows.append(e / jnp.sum(e, axis=0, keepdims=True))
        for pos in top_pos:
            a = jnp.floor(pos * (1.0 / PEER_TOPK))
            b = pos - a * PEER_TOPK
            i1_rows.append(jnp.sum(jnp.where(arow == a, i1m, 0.0), axis=0, keepdims=True))
            i2_rows.append(jnp.sum(jnp.where(arow == b, i2m, 0.0), axis=0, keepdims=True))
    i1_ref[...] = jnp.concatenate(i1_rows, axis=0).T
    i2_ref[...] = jnp.concatenate(i2_rows, axis=0).T
    gate_ref[...] = jnp.concatenate(gate_rows, axis=0).T


def _route(x1, wq_bf, keys_bf, *, tb=128):
    t = x1.shape[0]
    hk = PEER_HEADS * PEER_TOPK
    row = lambda i: (i, 0)
    ids = _peer_candidate_ids()
    cand_ids = jnp.asarray(np.ascontiguousarray(np.broadcast_to(ids[:, None], (ids.shape[0], tb))))
    return pl.pallas_call(
        _route_kernel,
        grid=(t // tb,),
        in_specs=[
            pl.BlockSpec((tb, D_MODEL), row),
            pl.BlockSpec((D_MODEL, PEER_HEADS * PEER_DQ), lambda i: (0, 0)),
            pl.BlockSpec((PEER_HEADS * 2, PEER_NKEYS, PEER_DQ // 2), lambda i: (0, 0, 0)),
            pl.BlockSpec(cand_ids.shape, lambda i: (0, 0)),
        ],
        out_specs=[pl.BlockSpec((tb, hk), row)] * 3,
        out_shape=[jax.ShapeDtypeStruct((t, hk), F32)] * 3,
        compiler_params=pltpu.CompilerParams(dimension_semantics=("arbitrary",),
                                             vmem_limit_bytes=VMEM_LIMIT_BYTES),
        name="peer_route",
    )(x1, wq_bf, keys_bf, cand_ids)


def _fp8_scale(bound):
    return jnp.exp2(jnp.floor(jnp.log2(FP8_TARGET / jnp.maximum(bound, FP8_TINY))))


def _peer_kernel(x_ref, i1_ref, i2_ref, gate_ref, ut_ref, stat_ref, v_ref, g_ref, b_ref, o_ref,
                 gs_ref, x8_ref, hscale_ref, wscale_ref, acc_ref, *, tb, ec, pitch, unroll):
    e = pl.program_id(1)
    ncl = ec // PEER_NKEYS

    @pl.when(e == 0)
    def _():
        x = x_ref[...]
        sx = _fp8_scale(jnp.max(jnp.max(jnp.abs(x), axis=0, keepdims=True), axis=1, keepdims=True))
        x8_ref[...] = (x * sx).astype(FP8)
        hscale_ref[...] = jnp.broadcast_to(stat_ref[0:1, 0:1] / sx, hscale_ref.shape)
        row_norm = jnp.sqrt(jnp.sum(x * x, axis=1, keepdims=True))
        sw = _fp8_scale(PEER_HEADS * row_norm * stat_ref[0:1, 1:2])
        wscale_ref[...] = jnp.broadcast_to(sw, wscale_ref.shape)
        acc_ref[...] = jnp.zeros_like(acc_ref)
        key_id = lax.broadcasted_iota(jnp.int32, (PEER_NKEYS, PEER_NKEYS), 0).astype(F32)

        def build(blk, carry):
            for r in range(unroll):
                t = blk * unroll + r
                i1 = jnp.broadcast_to(i1_ref[pl.ds(t, 1), :], (PEER_NKEYS, PEER_NKEYS))
                i2 = jnp.broadcast_to(i2_ref[pl.ds(t, 1), :], (PEER_NKEYS, PEER_NKEYS))
                gt = jnp.broadcast_to(0.5 * gate_ref[pl.ds(t, 1), :] * wscale_ref[pl.ds(t, 1), :]
                                      * hscale_ref[0:1, :], (PEER_NKEYS, PEER_NKEYS))
                a_t = jnp.where(i1 == key_id, gt, 0.0).astype(BF16)
                b_t = jnp.where(i2 == key_id, 1.0, 0.0).astype(BF16)
                g_tok = lax.dot_general(a_t, b_t, (((1,), (1,)), ((), ())), preferred_element_type=F32)
                g_rnd = g_tok.astype(BF16).astype(F32).reshape(PEER_NKEYS // PEER_PAIR, PEER_PAIR, PEER_NKEYS)
                hi = lax.bitcast_convert_type(g_rnd[:, :SUBLANES, :].reshape(PEER_PLANES, PEER_NKEYS), jnp.uint32)
                lo = lax.bitcast_convert_type(g_rnd[:, SUBLANES:, :].reshape(PEER_PLANES, PEER_NKEYS),
                                              jnp.uint32) >> 16
                gs_ref[pl.ds(t, PEER_PLANES, stride=pitch), :] = hi | lo
            return carry

        lax.fori_loop(0, tb // unroll, build, 0)

    d = jnp.dot(x8_ref[...], ut_ref[...], preferred_element_type=F32)
    act = d * (1.0 + lax.erf(d * (hscale_ref[0:1, 0:1] * 2.0 ** -0.5)))
    parts = [None] * ncl
    for grp in range(ncl // PEER_PAIR):
        for i in range(SUBLANES):
            plane = (e * (ncl // PEER_PAIR) + grp) * SUBLANES + i
            word = gs_ref[pl.ds(pl.multiple_of(plane * pitch, SUBLANES), tb), :]
            halves = (lax.bitcast_convert_type(word & jnp.uint32(0xFFFF0000), F32),
                      lax.bitcast_convert_type(word << 16, F32))
            for half, gval in enumerate(halves):
                cl = grp * PEER_PAIR + half * SUBLANES + i
                parts[cl] = act[:, cl * PEER_NKEYS:(cl + 1) * PEER_NKEYS] * gval
    w = jnp.concatenate(parts, axis=1).astype(FP8)
    acc_ref[...] += jnp.dot(w, v_ref[...], preferred_element_type=F32)

    @pl.when(e == pl.num_programs(1) - 1)
    def _():
        peer = acc_ref[...] * (stat_ref[0:1, 2:3] / wscale_ref[:, 0:1])
        o_ref[...] = _layer_norm_rows(DEEPNORM_ALPHA * x_ref[...] + peer, g_ref[...], b_ref[...])


def _peer(x1, i1, i2, gate, ut_f8, stats, v_f8, ln_g, ln_b, *, tb=512, ec=2048, unroll=32):
    t = x1.shape[0]
    n_exp = v_f8.shape[0]
    hk = PEER_HEADS * PEER_TOPK
    pitch = tb + SUBLANES
    row = lambda i, e: (i, 0)
    const = lambda i, e: (0, 0)
    return pl.pallas_call(
        functools.partial(_peer_kernel, tb=tb, ec=ec, pitch=pitch, unroll=unroll),
        grid=(t // tb, n_exp // ec),
        in_specs=[
            pl.BlockSpec((tb, D_MODEL), row),
            pl.BlockSpec((tb, hk), row),
            pl.BlockSpec((tb, hk), row),
            pl.BlockSpec((tb, hk), row),
            pl.BlockSpec((D_MODEL, ec), lambda i, e: (0, e)),
            pl.BlockSpec((1, LANES), const),
            pl.BlockSpec((ec, D_MODEL), lambda i, e: (e, 0)),
            pl.BlockSpec((1, D_MODEL), const),
            pl.BlockSpec((1, D_MODEL), const),
        ],
        out_specs=pl.BlockSpec((tb, D_MODEL), row),
        out_shape=jax.ShapeDtypeStruct((t, D_MODEL), F32),
        scratch_shapes=[
            pltpu.VMEM((PEER_PLANES * pitch, PEER_NKEYS), jnp.uint32),
            pltpu.VMEM((tb, D_MODEL), FP8),
            pltpu.VMEM((SUBLANES, LANES), F32),
            pltpu.VMEM((tb, LANES), F32),
            pltpu.VMEM((tb, D_MODEL), F32),
        ],
        compiler_params=pltpu.CompilerParams(dimension_semantics=("arbitrary", "arbitrary"),
                                             vmem_limit_bytes=VMEM_LIMIT_BYTES),
        name="peer_dense",
    )(x1, i1, i2, gate, ut_f8, stats, v_f8, ln_g, ln_b)


def _rope_tables(seq):
    rows = seq // GRID_W
    row = jnp.repeat(jnp.arange(rows, dtype=F32), GRID_W)
    col = jnp.tile(jnp.arange(GRID_W, dtype=F32), rows)
    n_freq = ATT_DH // 4
    inv_freq = ROPE_THETA ** (-jnp.arange(n_freq, dtype=F32) / n_freq)
    ang = jnp.concatenate([row[:, None] * inv_freq, col[:, None] * inv_freq], axis=-1)
    cos, sin = jnp.cos(ang), jnp.sin(ang)
    return jnp.concatenate([cos, cos], axis=-1), jnp.concatenate([-sin, sin], axis=-1)


def _layer(x2d, batch, seq, cos2, sin2, p):
    rq, rk, rv, rg, aq, ak, av, y_sgu, gates = _inproj(
        x2d, seq, p["w_in"], cos2, sin2, p["b_gate"], p["qn"], p["kn"], p["sgu_ln_g"], p["sgu_ln_b"],
        p["sgu_w"], p["sgu_bt"])
    y_ret = _retention(rq, rk, rv, rg, p["lg"], batch, seq)
    y_att = _attention(aq, ak, jnp.transpose(av), p["att_score_bound"], batch, seq)
    x1 = _merge(x2d, y_ret, y_att, y_sgu, gates, p["w_branch"], p["w_out"], p["ln1_g"], p["ln1_b"])
    i1, i2, gate = _route(x1, p["peer_wq"], p["peer_keys"])
    return _peer(x1, i1, i2, gate, p["peer_ut"], p["peer_stats"], p["peer_v"], p["ln2_g"], p["ln2_b"])


def _prep_layer(l, w_in, b_gate, ret_decay_fwd, ret_decay_bwd, attn_q_norm, attn_k_norm,
                sgu_ln_g, sgu_ln_b, sgu_w, sgu_b, w_branch, w_out, ln1_g, ln1_b,
                peer_wq, peer_keys, peer_u, peer_v, ln2_g, ln2_b):
    half = PEER_DQ // 2
    u_scale = _fp8_scale(jnp.max(jnp.abs(peer_u[l])))
    v_scale = _fp8_scale(jnp.max(jnp.abs(peer_v[l])))
    u_norm = jnp.sqrt(jnp.max(jnp.sum(jnp.square(peer_u[l]), axis=1)))
    stats = jnp.zeros((1, LANES), F32).at[0, 0].set(1.0 / u_scale).at[0, 1].set(u_norm).at[0, 2].set(1.0 / v_scale)
    return dict(
        w_in=w_in[l].astype(BF16),
        b_gate=b_gate[l].reshape(1, GATE_W),
        lg=jnp.stack([jax.nn.log_sigmoid(ret_decay_fwd[l].astype(F32)),
                      jax.nn.log_sigmoid(ret_decay_bwd[l].astype(F32))]),
        att_score_bound=(ATT_DH * ATT_Q_SCALE * ATT_BOUND_SLACK
                         * jnp.max(jnp.abs(attn_q_norm[l])) * jnp.max(jnp.abs(attn_k_norm[l]))),
        qn=attn_q_norm[l].reshape(1, ATT_DH),
        kn=attn_k_norm[l].reshape(1, ATT_DH),
        sgu_ln_g=sgu_ln_g[l].reshape(1, SGU_WIDTH),
        sgu_ln_b=sgu_ln_b[l].reshape(1, SGU_WIDTH),
        sgu_w=sgu_w[l].astype(BF16),
        sgu_bt=jnp.transpose(sgu_b[l]),
        w_branch=w_branch[l].astype(BF16),
        w_out=w_out[l].astype(BF16),
        ln1_g=ln1_g[l].reshape(1, D_MODEL),
        ln1_b=ln1_b[l].reshape(1, D_MODEL),
        peer_wq=peer_wq[l].astype(BF16),
        peer_keys=peer_keys[l].reshape(PEER_HEADS * 2, PEER_NKEYS, half).astype(BF16),
        peer_ut=jnp.transpose(peer_u[l] * u_scale).astype(FP8),
        peer_stats=stats,
        peer_v=(peer_v[l] * v_scale).astype(FP8),
        ln2_g=ln2_g[l].reshape(1, D_MODEL),
        ln2_b=ln2_b[l].reshape(1, D_MODEL),
    )


def kernel(x_prompt, x_sample, w_in, b_gate, ret_decay_fwd, ret_decay_bwd, attn_q_norm, attn_k_norm,
           sgu_ln_g, sgu_ln_b, sgu_w, sgu_b, w_branch, w_out, ln1_g, ln1_b,
           peer_wq, peer_keys, peer_u, peer_v, ln2_g, ln2_b):
    weights = (w_in, b_gate, ret_decay_fwd, ret_decay_bwd, attn_q_norm, attn_k_norm,
               sgu_ln_g, sgu_ln_b, sgu_w, sgu_b, w_branch, w_out, ln1_g, ln1_b,
               peer_wq, peer_keys, peer_u, peer_v, ln2_g, ln2_b)
    layers = [_prep_layer(l, *weights) for l in range(DEPTH)]
    outs = []
    for x in (x_prompt, x_sample):
        batch, seq, _ = x.shape
        cos2, sin2 = _rope_tables(seq)
        h = x.reshape(batch * seq, D_MODEL)
        for p in layers:
            h = _layer(h, batch, seq, cos2, sin2, p)
        outs.append(h.reshape(batch, seq, D_MODEL))
    return tuple(outs)
```

```python
import functools
import math

import jax
import jax.numpy as jnp
import numpy as np
from jax import lax
from jax.experimental import pallas as pl
from jax.experimental.pallas import tpu as pltpu

F32 = jnp.float32
BF16 = jnp.bfloat16
FP8 = jnp.float8_e4m3fn

D_MODEL = 1024
DEPTH = 2
GRID_W = 64
CHUNK = 128
RET_HEADS = 4
RET_DK = 128
RET_DV = 256
ATT_HEADS = 8
ATT_KV_HEADS = 2
ATT_DH = 128
ATT_GROUP = ATT_HEADS // ATT_KV_HEADS
SGU_GROUPS = 4
SGU_WIDTH = 1024
SGU_GW = SGU_WIDTH // SGU_GROUPS
PEER_HEADS = 8
PEER_NKEYS = 128
PEER_DQ = 256
PEER_TOPK = 16
PEER_PLANES = PEER_NKEYS // 2
PEER_PAIR = 16
N_BRANCH = 3
ROPE_THETA = 10000.0
DEEPNORM_ALPHA = (2 * DEPTH) ** 0.25
LN_EPS = 1e-5
RMS_EPS = 1e-6

RET_QK_W = RET_HEADS * RET_DK
RET_V_W = RET_HEADS * RET_DV
ATT_Q_W = ATT_HEADS * ATT_DH
ATT_KV_W = ATT_KV_HEADS * ATT_DH
GATE_W = N_BRANCH * D_MODEL
SPLITS = (RET_QK_W, RET_QK_W, RET_V_W, RET_V_W, ATT_Q_W, ATT_KV_W, ATT_KV_W, SGU_WIDTH, SGU_WIDTH, GATE_W)
OFFS = tuple(int(sum(SPLITS[:i])) for i in range(len(SPLITS) + 1))
IN_W = OFFS[-1]

LANES = 128
SUBLANES = 8
VMEM_LIMIT_BYTES = 56 * 1024 * 1024

NEG_INF = float("-inf")
FP8_TARGET = 224.0
FP8_TINY = 1e-30
ATT_Q_SCALE = ATT_DH ** -0.5 * math.log2(math.e)
ATT_LOGIT_LIMIT = 64.0
ATT_BOUND_SLACK = 1.02


def _gelu(x):
    return 0.5 * x * (1.0 + lax.erf(x * (2.0 ** -0.5)))


def _rope(x, cos2, sin2):
    return x * cos2 + pltpu.roll(x, ATT_DH // 2, 1) * sin2


def _layer_norm_rows(x, g, b):
    mu = jnp.mean(x, axis=-1, keepdims=True)
    xc = x - mu
    var = jnp.mean(xc * xc, axis=-1, keepdims=True)
    return xc * lax.rsqrt(var + LN_EPS) * g + b


def _inproj_kernel(x_ref, w_ref, cos_ref, sin_ref, bgate_ref, qn_ref, kn_ref, lng_ref, lnb_ref,
                   sw_ref, sb_ref,
                   rq_ref, rk_ref, rv_ref, rg_ref, aq_ref, ak_ref, av_ref, ysgu_ref, gates_ref):
    tb = x_ref.shape[0]
    xb = x_ref[...].astype(BF16)
    cos2 = cos_ref[...]
    sin2 = sin_ref[...]

    def proj(i):
        return jnp.dot(xb, w_ref[:, OFFS[i]:OFFS[i + 1]], preferred_element_type=F32)

    h = proj(0)
    for hd in range(RET_HEADS):
        sl = slice(hd * RET_DK, (hd + 1) * RET_DK)
        rq_ref[:, sl] = _rope(h[:, sl], cos2, sin2).astype(BF16)
    h = proj(1)
    for hd in range(RET_HEADS):
        sl = slice(hd * RET_DK, (hd + 1) * RET_DK)
        rk_ref[:, sl] = (_rope(h[:, sl], cos2, sin2) * (RET_DK ** -0.5)).astype(BF16)
    rv_ref[...] = proj(2).astype(BF16)
    h = proj(3)
    rg_ref[...] = (h * jax.nn.sigmoid(h)).astype(BF16)

    h = proj(4)
    qn = qn_ref[...]
    for hd in range(ATT_HEADS):
        sl = slice(hd * ATT_DH, (hd + 1) * ATT_DH)
        t = h[:, sl]
        t = t * lax.rsqrt(jnp.mean(t * t, axis=-1, keepdims=True) + RMS_EPS) * qn
        aq_ref[:, sl] = (_rope(t, cos2, sin2) * ATT_Q_SCALE).astype(BF16)
    h = proj(5)
    kn = kn_ref[...]
    for hd in range(ATT_KV_HEADS):
        sl = slice(hd * ATT_DH, (hd + 1) * ATT_DH)
        t = h[:, sl]
        t = t * lax.rsqrt(jnp.mean(t * t, axis=-1, keepdims=True) + RMS_EPS) * kn
        ak_ref[:, sl] = _rope(t, cos2, sin2).astype(BF16)
    av_ref[...] = proj(6).astype(BF16)

    u = _gelu(proj(7))
    vn = _layer_norm_rows(_gelu(proj(8)), lng_ref[...], lnb_ref[...]).astype(BF16)
    sb = sb_ref[...]
    for c in range(tb // CHUNK):
        rows = slice(c * CHUNK, (c + 1) * CHUNK)
        for g in range(SGU_GROUPS):
            cols = slice(g * SGU_GW, (g + 1) * SGU_GW)
            mixed = jnp.dot(sw_ref[g], vn[rows, cols], preferred_element_type=F32) + sb[:, g:g + 1]
            ysgu_ref[rows, cols] = (u[rows, cols] * mixed).astype(BF16)

    gates_ref[...] = jax.nn.sigmoid(proj(9) + bgate_ref[...]).astype(BF16)


def _inproj(x2d, seq, w_bf, cos2, sin2, b_gate, qn, kn, lng, lnb, sgu_w_bf, sgu_bt, *, tb=256):
    t = x2d.shape[0]
    nblk_seq = seq // tb
    const = lambda i: (0, 0)
    row = lambda i: (i, 0)
    pos = lambda i: (i % nblk_seq, 0)
    outs = [(RET_QK_W, BF16), (RET_QK_W, BF16), (RET_V_W, BF16), (RET_V_W, BF16), (ATT_Q_W, BF16),
            (ATT_KV_W, BF16), (ATT_KV_W, BF16), (SGU_WIDTH, BF16), (GATE_W, BF16)]
    return pl.pallas_call(
        _inproj_kernel,
        grid=(t // tb,),
        in_specs=[
            pl.BlockSpec((tb, D_MODEL), row),
            pl.BlockSpec((D_MODEL, IN_W), const, pipeline_mode=pl.Buffered(1)),
            pl.BlockSpec((tb, ATT_DH), pos),
            pl.BlockSpec((tb, ATT_DH), pos),
            pl.BlockSpec((1, GATE_W), const),
            pl.BlockSpec((1, ATT_DH), const),
            pl.BlockSpec((1, ATT_DH), const),
            pl.BlockSpec((1, SGU_WIDTH), const),
            pl.BlockSpec((1, SGU_WIDTH), const),
            pl.BlockSpec((SGU_GROUPS, CHUNK, CHUNK), lambda i: (0, 0, 0)),
            pl.BlockSpec((CHUNK, SGU_GROUPS), const),
        ],
        out_specs=[pl.BlockSpec((tb, w), row) for w, _ in outs],
        out_shape=[jax.ShapeDtypeStruct((t, w), d) for w, d in outs],
        compiler_params=pltpu.CompilerParams(dimension_semantics=("arbitrary",),
                                             vmem_limit_bytes=VMEM_LIMIT_BYTES),
        name="inproj",
    )(x2d, w_bf, cos2, sin2, b_gate, qn, kn, lng, lnb, sgu_w_bf, sgu_bt)


def _retention_kernel(lg_ref, q_ref, k_ref, v_ref, *rest, rb, forward):
    if forward:
        g_ref, yb_ref, o_ref, state_ref = rest
    else:
        o_ref, state_ref = rest
    nchunk = rb // CHUNK

    @pl.when(pl.program_id(1) == 0)
    def _():
        state_ref[...] = jnp.zeros_like(state_ref)

    n_i = lax.broadcasted_iota(jnp.int32, (CHUNK, CHUNK), 0)
    m_i = lax.broadcasted_iota(jnp.int32, (CHUNK, CHUNK), 1)
    col = lax.broadcasted_iota(jnp.int32, (CHUNK, 1), 0).astype(F32)
    dist = ((n_i - m_i) if forward else (m_i - n_i)).astype(F32)
    live = (n_i >= m_i) if forward else (m_i > n_i)

    for hd in range(RET_HEADS):
        lg = lg_ref[0 if forward else 1, hd]
        decay = jnp.where(live, jnp.exp(lg * jnp.maximum(dist, 0.0)), 0.0)
        zeta = jnp.exp(lg * ((CHUNK - 1.0 - col) if forward else col))
        xi = jnp.exp(lg * ((col + 1.0) if forward else (CHUNK - col)))
        chunk_decay = jnp.exp(lg * CHUNK)
        qk_cols = slice(hd * RET_DK, (hd + 1) * RET_DK)
        v_cols = slice(hd * RET_DV, (hd + 1) * RET_DV)
        for c in (range(nchunk) if forward else reversed(range(nchunk))):
            rows = slice(c * CHUNK, (c + 1) * CHUNK)
            q = q_ref[rows, qk_cols]
            k = k_ref[rows, qk_cols]
            v = v_ref[rows, v_cols]
            sc = lax.dot_general(q, k, (((1,), (1,)), ((), ())), preferred_element_type=F32) * decay
            st = state_ref[hd]
            lhs = jnp.concatenate([sc.astype(BF16), (q.astype(F32) * xi).astype(BF16)], axis=1)
            y = jnp.dot(lhs, jnp.concatenate([v, st.astype(BF16)], axis=0), preferred_element_type=F32)
            vz = (v.astype(F32) * zeta).astype(BF16)
            upd = lax.dot_general(k, vz, (((0,), (0,)), ((), ())), preferred_element_type=F32)
            state_ref[hd] = st * chunk_decay + upd
            if forward:
                y = y + yb_ref[rows, v_cols]
                mu = jnp.mean(y, axis=-1, keepdims=True)
                yc = y - mu
                var = jnp.mean(yc * yc, axis=-1, keepdims=True)
                yn = yc * lax.rsqrt(var + LN_EPS)
                o_ref[rows, v_cols] = (g_ref[rows, v_cols].astype(F32) * yn).astype(BF16)
            else:
                o_ref[rows, v_cols] = y


def _retention_pass(lg, rq, rk, rv, extra, batch, seq, *, rb, forward):
    t = rq.shape[0]
    ns = seq // rb
    if forward:
        blk = lambda b, s, lg_ref: (b * ns + s, 0)
    else:
        blk = lambda b, s, lg_ref: (b * ns + ns - 1 - s, 0)
    grid_spec = pltpu.PrefetchScalarGridSpec(
        num_scalar_prefetch=1,
        grid=(batch, ns),
        in_specs=[pl.BlockSpec((rb, RET_QK_W), blk), pl.BlockSpec((rb, RET_QK_W), blk),
                  pl.BlockSpec((rb, RET_V_W), blk)] + [pl.BlockSpec((rb, RET_V_W), blk)] * len(extra),
        out_specs=pl.BlockSpec((rb, RET_V_W), blk),
        scratch_shapes=[pltpu.VMEM((RET_HEADS, RET_DK, RET_DV), F32)],
    )
    return pl.pallas_call(
        functools.partial(_retention_kernel, rb=rb, forward=forward),
        grid_spec=grid_spec,
        out_shape=jax.ShapeDtypeStruct((t, RET_V_W), BF16 if forward else F32),
        compiler_params=pltpu.CompilerParams(dimension_semantics=("arbitrary", "arbitrary"),
                                             vmem_limit_bytes=VMEM_LIMIT_BYTES),
        name="retention_fwd" if forward else "retention_bwd",
    )(lg, rq, rk, rv, *extra)


def _retention(rq, rk, rv, rg, lg, batch, seq, *, rb=512):
    y_bwd = _retention_pass(lg, rq, rk, rv, (), batch, seq, rb=rb, forward=False)
    return _retention_pass(lg, rq, rk, rv, (rg, y_bwd), batch, seq, rb=rb, forward=True)


def _attention_kernel(q_ref, k_ref, vt_ref, o_ref, m_ref, l_ref, acc_ref, *, bounded):
    ki = pl.program_id(3)

    @pl.when(ki == 0)
    def _():
        if not bounded:
            m_ref[...] = jnp.full_like(m_ref, NEG_INF)
        l_ref[...] = jnp.zeros_like(l_ref)
        acc_ref[...] = jnp.zeros_like(acc_ref)

    k = k_ref[...]
    vt = vt_ref[...]
    def scores(g):
        q = q_ref[:, g * ATT_DH:(g + 1) * ATT_DH]
        return lax.dot_general(k, q, (((1,), (1,)), ((), ())), preferred_element_type=F32)

    st_next = scores(0)
    for g in range(ATT_GROUP):
        st = st_next
        if g + 1 < ATT_GROUP:
            st_next = scores(g + 1)
        if bounded:
            pt = jnp.exp2(st)
            l_ref[g:g + 1, :] += jnp.sum(pt, axis=0, keepdims=True)
            acc_ref[g] += jnp.dot(vt, pt.astype(BF16), preferred_element_type=F32)
            continue
        m_prev = m_ref[g:g + 1, :]
        m_new = jnp.maximum(m_prev, jnp.max(st, axis=0, keepdims=True))
        alpha = jnp.exp2(m_prev - m_new)
        pt = jnp.exp2(st - m_new)
        l_ref[g:g + 1, :] = alpha * l_ref[g:g + 1, :] + jnp.sum(pt, axis=0, keepdims=True)
        acc_ref[g] = alpha * acc_ref[g] + jnp.dot(vt, pt.astype(BF16), preferred_element_type=F32)
        m_ref[g:g + 1, :] = m_new

    @pl.when(ki == pl.num_programs(3) - 1)
    def _():
        for g in range(ATT_GROUP):
            out = acc_ref[g] / l_ref[g:g + 1, :]
            o_ref[:, g * ATT_DH:(g + 1) * ATT_DH] = out.T.astype(BF16)


def _attention(aq, ak, avt, score_bound, batch, seq):
    return lax.cond(score_bound <= ATT_LOGIT_LIMIT,
                    functools.partial(_attention_call, batch=batch, seq=seq, bounded=True),
                    functools.partial(_attention_call, batch=batch, seq=seq, bounded=False),
                    aq, ak, avt)


def _attention_call(aq, ak, avt, *, batch, seq, bounded, tq=512, tk=4096):
    t = aq.shape[0]
    nq = seq // tq
    nk = seq // tk
    gw = ATT_GROUP * ATT_DH
    return pl.pallas_call(
        functools.partial(_attention_kernel, bounded=bounded),
        grid=(batch, ATT_KV_HEADS, nq, nk),
        in_specs=[
            pl.BlockSpec((tq, gw), lambda b, g, qi, ki: (b * nq + qi, g)),
            pl.BlockSpec((tk, ATT_DH), lambda b, g, qi, ki: (b * nk + ki, g)),
            pl.BlockSpec((ATT_DH, tk), lambda b, g, qi, ki: (g, b * nk + ki)),
        ],
        out_specs=pl.BlockSpec((tq, gw), lambda b, g, qi, ki: (b * nq + qi, g)),
        out_shape=jax.ShapeDtypeStruct((t, ATT_Q_W), BF16),
        scratch_shapes=[
            pltpu.VMEM((SUBLANES, tq), F32),
            pltpu.VMEM((SUBLANES, tq), F32),
            pltpu.VMEM((ATT_GROUP, ATT_DH, tq), F32),
        ],
        compiler_params=pltpu.CompilerParams(
            dimension_semantics=("arbitrary", "arbitrary", "arbitrary", "arbitrary"),
            vmem_limit_bytes=VMEM_LIMIT_BYTES),
        name="attention_bounded" if bounded else "attention",
    )(aq, ak, avt)


def _merge_kernel(x_ref, yr_ref, ya_ref, ys_ref, gates_ref, wb_ref, wo_ref, g_ref, b_ref, o_ref):
    merged = None
    for i, y_ref in enumerate((yr_ref, ya_ref, ys_ref)):
        gate = gates_ref[:, i * D_MODEL:(i + 1) * D_MODEL].astype(F32)
        term = gate * jnp.dot(y_ref[...], wb_ref[i], preferred_element_type=F32)
        merged = term if merged is None else merged + term
    proj = jnp.dot(merged.astype(BF16), wo_ref[...], preferred_element_type=F32)
    o_ref[...] = _layer_norm_rows(DEEPNORM_ALPHA * x_ref[...] + proj, g_ref[...], b_ref[...])


def _merge(x2d, y_ret, y_att, y_sgu, gates, wb_bf, wo_bf, ln_g, ln_b, *, tb=512):
    t = x2d.shape[0]
    row = lambda i: (i, 0)
    const = lambda i: (0, 0)
    return pl.pallas_call(
        _merge_kernel,
        grid=(t // tb,),
        in_specs=[
            pl.BlockSpec((tb, D_MODEL), row),
            pl.BlockSpec((tb, D_MODEL), row),
            pl.BlockSpec((tb, D_MODEL), row),
            pl.BlockSpec((tb, D_MODEL), row),
            pl.BlockSpec((tb, GATE_W), row),
            pl.BlockSpec((N_BRANCH, D_MODEL, D_MODEL), lambda i: (0, 0, 0)),
            pl.BlockSpec((D_MODEL, D_MODEL), const),
            pl.BlockSpec((1, D_MODEL), const),
            pl.BlockSpec((1, D_MODEL), const),
        ],
        out_specs=pl.BlockSpec((tb, D_MODEL), row),
        out_shape=jax.ShapeDtypeStruct((t, D_MODEL), F32),
        compiler_params=pltpu.CompilerParams(dimension_semantics=("arbitrary",),
                                             vmem_limit_bytes=VMEM_LIMIT_BYTES),
        name="merge",
    )(x2d, y_ret, y_att, y_sgu, gates, wb_bf, wo_bf, ln_g, ln_b)


PEER_PAD_ID = 1024.0


def _peer_candidate_ids():
    pairs = [(0, b) for b in range(PEER_TOPK)]
    pairs += [(a, b) for a in range(1, SUBLANES) for b in range(SUBLANES)]
    pairs += [(a, 0) for a in range(SUBLANES, PEER_TOPK)]
    ids = [float(a * PEER_TOPK + b) if (a + 1) * (b + 1) <= PEER_TOPK else PEER_PAD_ID for a, b in pairs]
    return np.asarray(ids, np.float32)


def _topk_rows(v, ids, k, group):
    nblk = v.shape[0] // SUBLANES
    lists = []
    for g0 in range(0, nblk, group):
        lv = [v[b * SUBLANES:(b + 1) * SUBLANES] for b in range(g0, g0 + group)]
        li = [ids[b * SUBLANES:(b + 1) * SUBLANES] for b in range(g0, g0 + group)]
        for rnd in range(group):
            for a in range(rnd % 2, group - 1, 2):
                swap = lv[a + 1] > lv[a]
                lv[a], lv[a + 1] = jnp.where(swap, lv[a + 1], lv[a]), jnp.where(swap, lv[a], lv[a + 1])
                li[a], li[a + 1] = jnp.where(swap, li[a + 1], li[a]), jnp.where(swap, li[a], li[a + 1])
        lists.append((lv, li))
    vals, idxs = [], []
    for _ in range(k):
        head = functools.reduce(jnp.maximum, [lv[0] for lv, _ in lists])
        m = jnp.max(head, axis=0, keepdims=True)
        cand = functools.reduce(jnp.minimum,
                                [jnp.where(lv[0] == m, li[0], 2.0 * PEER_PAD_ID) for lv, li in lists])
        idx = jnp.min(cand, axis=0, keepdims=True)
        vals.append(m)
        idxs.append(idx)
        for lv, li in lists:
            won = li[0] == idx
            for j in range(group - 1):
                lv[j] = jnp.where(won, lv[j + 1], lv[j])
                li[j] = jnp.where(won, li[j + 1], li[j])
            lv[group - 1] = jnp.where(won, NEG_INF, lv[group - 1])
    return vals, idxs


def _route_kernel(x_ref, wq_ref, keys_ref, cid_ref, i1_ref, i2_ref, gate_ref):
    tb = x_ref.shape[0]
    q = jnp.dot(x_ref[...].astype(BF16), wq_ref[...], preferred_element_type=F32).astype(BF16)
    half = PEER_DQ // 2
    key_row = lax.broadcasted_iota(jnp.int32, (PEER_NKEYS, tb), 0).astype(F32)
    arow = lax.broadcasted_iota(jnp.int32, (PEER_TOPK, tb), 0).astype(F32)
    cand_ids = cid_ref[...]
    i1_rows, i2_rows, gate_rows = [], [], []
    for h in range(PEER_HEADS):
        sub = []
        for p in range(2):
            qs = q[:, (h * 2 + p) * half:(h * 2 + p + 1) * half]
            sc = lax.dot_general(keys_ref[h * 2 + p], qs, (((1,), (1,)), ((), ())),
                                 preferred_element_type=F32)
            sub.append(_topk_rows(sc, key_row, PEER_TOPK, group=8))
        (s1, i1), (s2, i2) = sub
        s1m = jnp.concatenate(s1, axis=0)
        s2m = jnp.concatenate(s2, axis=0)
        i1m = jnp.concatenate(i1, axis=0)
        i2m = jnp.concatenate(i2, axis=0)
        pieces = [s1[0] + s2m[:SUBLANES], s1[0] + s2m[SUBLANES:]]
        pieces += [s1[a] + s2m[:SUBLANES] for a in range(1, SUBLANES)]
        pieces += [s1m[SUBLANES:] + s2[0]]
        cand = jnp.where(cand_ids < PEER_PAD_ID, jnp.concatenate(pieces, axis=0), NEG_INF)
        top_s, top_pos = _topk_rows(cand, cand_ids, PEER_TOPK, group=5)
        top_s = jnp.concatenate(top_s, axis=0)
        e = jnp.exp(top_s - top_s[0:1, :])
        gate_rows.append(e / jnp.sum(e, axis=0, keepdims=True))
        for pos in top_pos:
            a = jnp.floor(pos * (1.0 / PEER_TOPK))
            b = pos - a * PEER_TOPK
            i1_rows.append(jnp.sum(jnp.where(arow == a, i1m, 0.0), axis=0, keepdims=True))
            i2_rows.append(jnp.sum(jnp.where(arow == b, i2m, 0.0), axis=0, keepdims=True))
    i1_ref[...] = jnp.concatenate(i1_rows, axis=0).T
    i2_ref[...] = jnp.concatenate(i2_rows, axis=0).T
    gate_ref[...] = jnp.concatenate(gate_rows, axis=0).T


def _route(x1, wq_bf, keys_bf, *, tb=128):
    t = x1.shape[0]
    hk = PEER_HEADS * PEER_TOPK
    row = lambda i: (i, 0)
    ids = _peer_candidate_ids()
    cand_ids = jnp.asarray(np.ascontiguousarray(np.broadcast_to(ids[:, None], (ids.shape[0], tb))))
    return pl.pallas_call(
        _route_kernel,
        grid=(t // tb,),
        in_specs=[
            pl.BlockSpec((tb, D_MODEL), row),
            pl.BlockSpec((D_MODEL, PEER_HEADS * PEER_DQ), lambda i: (0, 0)),
            pl.BlockSpec((PEER_HEADS * 2, PEER_NKEYS, PEER_DQ // 2), lambda i: (0, 0, 0)),
            pl.BlockSpec(cand_ids.shape, lambda i: (0, 0)),
        ],
        out_specs=[pl.BlockSpec((tb, hk), row)] * 3,
        out_shape=[jax.ShapeDtypeStruct((t, hk), F32)] * 3,
        compiler_params=pltpu.CompilerParams(dimension_semantics=("arbitrary",),
                                             vmem_limit_bytes=VMEM_LIMIT_BYTES),
        name="peer_route",
    )(x1, wq_bf, keys_bf, cand_ids)


def _fp8_scale(bound):
    return jnp.exp2(jnp.floor(jnp.log2(FP8_TARGET / jnp.maximum(bound, FP8_TINY))))


def _peer_kernel(x_ref, i1_ref, i2_ref, gate_ref, ut_ref, stat_ref, v_ref, g_ref, b_ref, o_ref,
                 gs_ref, x8_ref, hscale_ref, wscale_ref, acc_ref, *, tb, ec, pitch, unroll):
    e = pl.program_id(1)
    ncl = ec // PEER_NKEYS

    @pl.when(e == 0)
    def _():
        x = x_ref[...]
        sx = _fp8_scale(jnp.max(jnp.max(jnp.abs(x), axis=0, keepdims=True), axis=1, keepdims=True))
        x8_ref[...] = (x * sx).astype(FP8)
        hscale_ref[...] = jnp.broadcast_to(stat_ref[0:1, 0:1] / sx, hscale_ref.shape)
        row_norm = jnp.sqrt(jnp.sum(x * x, axis=1, keepdims=True))
        sw = _fp8_scale(PEER_HEADS * row_norm * stat_ref[0:1, 1:2])
        wscale_ref[...] = jnp.broadcast_to(sw, wscale_ref.shape)
        acc_ref[...] = jnp.zeros_like(acc_ref)
        key_id = lax.broadcasted_iota(jnp.int32, (PEER_NKEYS, PEER_NKEYS), 0).astype(F32)

        def build(blk, carry):
            for r in range(unroll):
                t = blk * unroll + r
                i1 = jnp.broadcast_to(i1_ref[pl.ds(t, 1), :], (PEER_NKEYS, PEER_NKEYS))
                i2 = jnp.broadcast_to(i2_ref[pl.ds(t, 1), :], (PEER_NKEYS, PEER_NKEYS))
                gt = jnp.broadcast_to(0.5 * gate_ref[pl.ds(t, 1), :] * wscale_ref[pl.ds(t, 1), :]
                                      * hscale_ref[0:1, :], (PEER_NKEYS, PEER_NKEYS))
                a_t = jnp.where(i1 == key_id, gt, 0.0).astype(BF16)
                b_t = jnp.where(i2 == key_id, 1.0, 0.0).astype(BF16)
                g_tok = lax.dot_general(a_t, b_t, (((1,), (1,)), ((), ())), preferred_element_type=F32)
                g_rnd = g_tok.astype(BF16).astype(F32).reshape(PEER_NKEYS // PEER_PAIR, PEER_PAIR, PEER_NKEYS)
                hi = lax.bitcast_convert_type(g_rnd[:, :SUBLANES, :].reshape(PEER_PLANES, PEER_NKEYS), jnp.uint32)
                lo = lax.bitcast_convert_type(g_rnd[:, SUBLANES:, :].reshape(PEER_PLANES, PEER_NKEYS),
                                              jnp.uint32) >> 16
                gs_ref[pl.ds(t, PEER_PLANES, stride=pitch), :] = hi | lo
            return carry

        lax.fori_loop(0, tb // unroll, build, 0)

    d = jnp.dot(x8_ref[...], ut_ref[...], preferred_element_type=F32)
    act = d * (1.0 + lax.erf(d * (hscale_ref[0:1, 0:1] * 2.0 ** -0.5)))
    parts = [None] * ncl
    for grp in range(ncl // PEER_PAIR):
        for i in range(SUBLANES):
            plane = (e * (ncl // PEER_PAIR) + grp) * SUBLANES + i
            word = gs_ref[pl.ds(pl.multiple_of(plane * pitch, SUBLANES), tb), :]
            halves = (lax.bitcast_convert_type(word & jnp.uint32(0xFFFF0000), F32),
                      lax.bitcast_convert_type(word << 16, F32))
            for half, gval in enumerate(halves):
                cl = grp * PEER_PAIR + half * SUBLANES + i
                parts[cl] = act[:, cl * PEER_NKEYS:(cl + 1) * PEER_NKEYS] * gval
    w = jnp.concatenate(parts, axis=1).astype(FP8)
    acc_ref[...] += jnp.dot(w, v_ref[...], preferred_element_type=F32)

    @pl.when(e == pl.num_programs(1) - 1)
    def _():
        peer = acc_ref[...] * (stat_ref[0:1, 2:3] / wscale_ref[:, 0:1])
        o_ref[...] = _layer_norm_rows(DEEPNORM_ALPHA * x_ref[...] + peer, g_ref[...], b_ref[...])


def _peer(x1, i1, i2, gate, ut_f8, stats, v_f8, ln_g, ln_b, *, tb=512, ec=2048, unroll=64):
    t = x1.shape[0]
    n_exp = v_f8.shape[0]
    hk = PEER_HEADS * PEER_TOPK
    pitch = tb + SUBLANES
    row = lambda i, e: (i, 0)
    const = lambda i, e: (0, 0)
    return pl.pallas_call(
        functools.partial(_peer_kernel, tb=tb, ec=ec, pitch=pitch, unroll=unroll),
        grid=(t // tb, n_exp // ec),
        in_specs=[
            pl.BlockSpec((tb, D_MODEL), row),
            pl.BlockSpec((tb, hk), row),
            pl.BlockSpec((tb, hk), row),
            pl.BlockSpec((tb, hk), row),
            pl.BlockSpec((D_MODEL, ec), lambda i, e: (0, e)),
            pl.BlockSpec((1, LANES), const),
            pl.BlockSpec((ec, D_MODEL), lambda i, e: (e, 0)),
            pl.BlockSpec((1, D_MODEL), const),
            pl.BlockSpec((1, D_MODEL), const),
        ],
        out_specs=pl.BlockSpec((tb, D_MODEL), row),
        out_shape=jax.ShapeDtypeStruct((t, D_MODEL), F32),
        scratch_shapes=[
            pltpu.VMEM((PEER_PLANES * pitch, PEER_NKEYS), jnp.uint32),
            pltpu.VMEM((tb, D_MODEL), FP8),
            pltpu.VMEM((SUBLANES, LANES), F32),
            pltpu.VMEM((tb, LANES), F32),
            pltpu.VMEM((tb, D_MODEL), F32),
        ],
        compiler_params=pltpu.CompilerParams(dimension_semantics=("arbitrary", "arbitrary"),
                                             vmem_limit_bytes=VMEM_LIMIT_BYTES),
        name="peer_dense",
    )(x1, i1, i2, gate, ut_f8, stats, v_f8, ln_g, ln_b)


def _rope_tables(seq):
    rows = seq // GRID_W
    row = jnp.repeat(jnp.arange(rows, dtype=F32), GRID_W)
    col = jnp.tile(jnp.arange(GRID_W, dtype=F32), rows)
    n_freq = ATT_DH // 4
    inv_freq = ROPE_THETA ** (-jnp.arange(n_freq, dtype=F32) / n_freq)
    ang = jnp.concatenate([row[:, None] * inv_freq, col[:, None] * inv_freq], axis=-1)
    cos, sin = jnp.cos(ang), jnp.sin(ang)
    return jnp.concatenate([cos, cos], axis=-1), jnp.concatenate([-sin, sin], axis=-1)


def _layer(x2d, batch, seq, cos2, sin2, p):
    rq, rk, rv, rg, aq, ak, av, y_sgu, gates = _inproj(
        x2d, seq, p["w_in"], cos2, sin2, p["b_gate"], p["qn"], p["kn"], p["sgu_ln_g"], p["sgu_ln_b"],
        p["sgu_w"], p["sgu_bt"])
    y_ret = _retention(rq, rk, rv, rg, p["lg"], batch, seq)
    y_att = _attention(aq, ak, jnp.transpose(av), p["att_score_bound"], batch, seq)
    x1 = _merge(x2d, y_ret, y_att, y_sgu, gates, p["w_branch"], p["w_out"], p["ln1_g"], p["ln1_b"])
    i1, i2, gate = _route(x1, p["peer_wq"], p["peer_keys"])
    return _peer(x1, i1, i2, gate, p["peer_ut"], p["peer_stats"], p["peer_v"], p["ln2_g"], p["ln2_b"])


def _prep_layer(l, w_in, b_gate, ret_decay_fwd, ret_decay_bwd, attn_q_norm, attn_k_norm,
                sgu_ln_g, sgu_ln_b, sgu_w, sgu_b, w_branch, w_out, ln1_g, ln1_b,
                peer_wq, peer_keys, peer_u, peer_v, ln2_g, ln2_b):
    half = PEER_DQ // 2
    u_norm = jnp.sqrt(jnp.max(jnp.sum(jnp.square(peer_u[l]), axis=1)))
    u_scale = _fp8_scale(u_norm)
    v_scale = _fp8_scale(jnp.max(jnp.abs(peer_v[l])))
    stats = jnp.zeros((1, LANES), F32).at[0, 0].set(1.0 / u_scale).at[0, 1].set(u_norm).at[0, 2].set(1.0 / v_scale)
    return dict(
        w_in=w_in[l].astype(BF16),
        b_gate=b_gate[l].reshape(1, GATE_W),
        lg=jnp.stack([jax.nn.log_sigmoid(ret_decay_fwd[l].astype(F32)),
                      jax.nn.log_sigmoid(ret_decay_bwd[l].astype(F32))]),
        att_score_bound=(ATT_DH * ATT_Q_SCALE * ATT_BOUND_SLACK
                         * jnp.max(jnp.abs(attn_q_norm[l])) * jnp.max(jnp.abs(attn_k_norm[l]))),
        qn=attn_q_norm[l].reshape(1, ATT_DH),
        kn=attn_k_norm[l].reshape(1, ATT_DH),
        sgu_ln_g=sgu_ln_g[l].reshape(1, SGU_WIDTH),
        sgu_ln_b=sgu_ln_b[l].reshape(1, SGU_WIDTH),
        sgu_w=sgu_w[l].astype(BF16),
        sgu_bt=jnp.transpose(sgu_b[l]),
        w_branch=w_branch[l].astype(BF16),
        w_out=w_out[l].astype(BF16),
        ln1_g=ln1_g[l].reshape(1, D_MODEL),
        ln1_b=ln1_b[l].reshape(1, D_MODEL),
        peer_wq=peer_wq[l].astype(BF16),
        peer_keys=peer_keys[l].reshape(PEER_HEADS * 2, PEER_NKEYS, half).astype(BF16),
        peer_ut=jnp.transpose(peer_u[l] * u_scale).astype(FP8),
        peer_stats=stats,
        peer_v=(peer_v[l] * v_scale).astype(FP8),
        ln2_g=ln2_g[l].reshape(1, D_MODEL),
        ln2_b=ln2_b[l].reshape(1, D_MODEL),
    )


def kernel(x_prompt, x_sample, w_in, b_gate, ret_decay_fwd, ret_decay_bwd, attn_q_norm, attn_k_norm,
           sgu_ln_g, sgu_ln_b, sgu_w, sgu_b, w_branch, w_out, ln1_g, ln1_b,
           peer_wq, peer_keys, peer_u, peer_v, ln2_g, ln2_b):
    weights = (w_in, b_gate, ret_decay_fwd, ret_decay_bwd, attn_q_norm, attn_k_norm,
               sgu_ln_g, sgu_ln_b, sgu_w, sgu_b, w_branch, w_out, ln1_g, ln1_b,
               peer_wq, peer_keys, peer_u, peer_v, ln2_g, ln2_b)
    layers = [_prep_layer(l, *weights) for l in range(DEPTH)]
    outs = []
    for x in (x_prompt, x_sample):
        batch, seq, _ = x.shape
        cos2, sin2 = _rope_tables(seq)
        h = x.reshape(batch * seq, D_MODEL)
        for p in layers:
            h = _layer(h, batch, seq, cos2, sin2, p)
        outs.append(h.reshape(batch, seq, D_MODEL))
    return tuple(outs)
```

```python
import functools
import math

import jax
import jax.numpy as jnp
import numpy as np
from jax import lax
from jax.experimental import pallas as pl
from jax.experimental.pallas import tpu as pltpu

F32 = jnp.float32
BF16 = jnp.bfloat16
FP8 = jnp.float8_e4m3fn

D_MODEL = 1024
DEPTH = 2
GRID_W = 64
CHUNK = 128
RET_HEADS = 4
RET_DK = 128
RET_DV = 256
ATT_HEADS = 8
ATT_KV_HEADS = 2
ATT_DH = 128
ATT_GROUP = ATT_HEADS // ATT_KV_HEADS
SGU_GROUPS = 4
SGU_WIDTH = 1024
SGU_GW = SGU_WIDTH // SGU_GROUPS
PEER_HEADS = 8
PEER_NKEYS = 128
PEER_DQ = 256
PEER_TOPK = 16
PEER_PLANES = PEER_NKEYS // 2
N_BRANCH = 3
ROPE_THETA = 10000.0
DEEPNORM_ALPHA = (2 * DEPTH) ** 0.25
LN_EPS = 1e-5
RMS_EPS = 1e-6

RET_QK_W = RET_HEADS * RET_DK
RET_V_W = RET_HEADS * RET_DV
ATT_Q_W = ATT_HEADS * ATT_DH
ATT_KV_W = ATT_KV_HEADS * ATT_DH
GATE_W = N_BRANCH * D_MODEL
SPLITS = (RET_QK_W, RET_QK_W, RET_V_W, RET_V_W, ATT_Q_W, ATT_KV_W, ATT_KV_W, SGU_WIDTH, SGU_WIDTH, GATE_W)
OFFS = tuple(int(sum(SPLITS[:i])) for i in range(len(SPLITS) + 1))
IN_W = OFFS[-1]

LANES = 128
SUBLANES = 8
PEER_PAIR = 2 * SUBLANES
VMEM_LIMIT_BYTES = 56 * 1024 * 1024

NEG_INF = float("-inf")
FP8_TARGET = 224.0
FP8_TINY = 1e-30
ATT_Q_SCALE = ATT_DH ** -0.5 * math.log2(math.e)
ATT_LOGIT_LIMIT = 64.0
ATT_BOUND_SLACK = 1.02


def _gelu(x):
    return 0.5 * x * (1.0 + lax.erf(x * (2.0 ** -0.5)))


def _rope(x, cos2, sin2):
    return x * cos2 + pltpu.roll(x, ATT_DH // 2, 1) * sin2


def _layer_norm_rows(x, g, b):
    mu = jnp.mean(x, axis=-1, keepdims=True)
    xc = x - mu
    var = jnp.mean(xc * xc, axis=-1, keepdims=True)
    return xc * lax.rsqrt(var + LN_EPS) * g + b


def _inproj_kernel(x_ref, w_ref, cos_ref, sin_ref, bgate_ref, qn_ref, kn_ref, lng_ref, lnb_ref,
                   sw_ref, sb_ref,
                   rq_ref, rk_ref, rv_ref, rg_ref, aq_ref, ak_ref, av_ref, ysgu_ref, gates_ref):
    tb = x_ref.shape[0]
    xb = x_ref[...].astype(BF16)
    cos2 = cos_ref[...]
    sin2 = sin_ref[...]

    def proj(i):
        return jnp.dot(xb, w_ref[:, OFFS[i]:OFFS[i + 1]], preferred_element_type=F32)

    gates_ref[...] = jax.nn.sigmoid(proj(9) + bgate_ref[...]).astype(BF16)

    u = _gelu(proj(7))
    vn = _layer_norm_rows(_gelu(proj(8)), lng_ref[...], lnb_ref[...]).astype(BF16)
    sb = sb_ref[...]
    for c in range(tb // CHUNK):
        rows = slice(c * CHUNK, (c + 1) * CHUNK)
        for g in range(SGU_GROUPS):
            cols = slice(g * SGU_GW, (g + 1) * SGU_GW)
            mixed = jnp.dot(sw_ref[g], vn[rows, cols], preferred_element_type=F32) + sb[:, g:g + 1]
            ysgu_ref[rows, cols] = (u[rows, cols] * mixed).astype(BF16)

    h = proj(4)
    qn = qn_ref[...]
    for hd in range(ATT_HEADS):
        sl = slice(hd * ATT_DH, (hd + 1) * ATT_DH)
        t = h[:, sl]
        t = t * lax.rsqrt(jnp.mean(t * t, axis=-1, keepdims=True) + RMS_EPS) * qn
        aq_ref[:, sl] = (_rope(t, cos2, sin2) * ATT_Q_SCALE).astype(BF16)
    h = proj(5)
    kn = kn_ref[...]
    for hd in range(ATT_KV_HEADS):
        sl = slice(hd * ATT_DH, (hd + 1) * ATT_DH)
        t = h[:, sl]
        t = t * lax.rsqrt(jnp.mean(t * t, axis=-1, keepdims=True) + RMS_EPS) * kn
        ak_ref[:, sl] = _rope(t, cos2, sin2).astype(BF16)
    av_ref[...] = proj(6).astype(BF16)

    h = proj(0)
    for hd in range(RET_HEADS):
        sl = slice(hd * RET_DK, (hd + 1) * RET_DK)
        rq_ref[:, sl] = _rope(h[:, sl], cos2, sin2).astype(BF16)
    h = proj(1)
    for hd in range(RET_HEADS):
        sl = slice(hd * RET_DK, (hd + 1) * RET_DK)
        rk_ref[:, sl] = (_rope(h[:, sl], cos2, sin2) * (RET_DK ** -0.5)).astype(BF16)
    rv_ref[...] = proj(2).astype(BF16)
    h = proj(3)
    rg_ref[...] = (h * jax.nn.sigmoid(h)).astype(BF16)


def _inproj(x2d, seq, w_bf, cos2, sin2, b_gate, qn, kn, lng, lnb, sgu_w_bf, sgu_bt, *, tb=256):
    t = x2d.shape[0]
    nblk_seq = seq // tb
    const = lambda i: (0, 0)
    row = lambda i: (i, 0)
    pos = lambda i: (i % nblk_seq, 0)
    outs = [(RET_QK_W, BF16), (RET_QK_W, BF16), (RET_V_W, BF16), (RET_V_W, BF16), (ATT_Q_W, BF16),
            (ATT_KV_W, BF16), (ATT_KV_W, BF16), (SGU_WIDTH, BF16), (GATE_W, BF16)]
    return pl.pallas_call(
        _inproj_kernel,
        grid=(t // tb,),
        in_specs=[
            pl.BlockSpec((tb, D_MODEL), row),
            pl.BlockSpec((D_MODEL, IN_W), const, pipeline_mode=pl.Buffered(1)),
            pl.BlockSpec((tb, ATT_DH), pos),
            pl.BlockSpec((tb, ATT_DH), pos),
            pl.BlockSpec((1, GATE_W), const),
            pl.BlockSpec((1, ATT_DH), const),
            pl.BlockSpec((1, ATT_DH), const),
            pl.BlockSpec((1, SGU_WIDTH), const),
            pl.BlockSpec((1, SGU_WIDTH), const),
            pl.BlockSpec((SGU_GROUPS, CHUNK, CHUNK), lambda i: (0, 0, 0)),
            pl.BlockSpec((CHUNK, SGU_GROUPS), const),
        ],
        out_specs=[pl.BlockSpec((tb, w), row) for w, _ in outs],
        out_shape=[jax.ShapeDtypeStruct((t, w), d) for w, d in outs],
        compiler_params=pltpu.CompilerParams(dimension_semantics=("arbitrary",),
                                             vmem_limit_bytes=VMEM_LIMIT_BYTES),
        name="inproj",
    )(x2d, w_bf, cos2, sin2, b_gate, qn, kn, lng, lnb, sgu_w_bf, sgu_bt)


def _retention_kernel(lg_ref, q_ref, k_ref, v_ref, *rest, rb, forward):
    if forward:
        g_ref, yb_ref, o_ref, state_ref = rest
    else:
        o_ref, state_ref = rest
    nchunk = rb // CHUNK

    @pl.when(pl.program_id(1) == 0)
    def _():
        state_ref[...] = jnp.zeros_like(state_ref)

    n_i = lax.broadcasted_iota(jnp.int32, (CHUNK, CHUNK), 0)
    m_i = lax.broadcasted_iota(jnp.int32, (CHUNK, CHUNK), 1)
    col = lax.broadcasted_iota(jnp.int32, (CHUNK, 1), 0).astype(F32)
    dist = ((n_i - m_i) if forward else (m_i - n_i)).astype(F32)
    live = (n_i >= m_i) if forward else (m_i > n_i)

    for hd in range(RET_HEADS):
        lg = lg_ref[0 if forward else 1, hd]
        decay = jnp.where(live, jnp.exp(lg * jnp.maximum(dist, 0.0)), 0.0)
        zeta = jnp.exp(lg * ((CHUNK - 1.0 - col) if forward else col))
        xi = jnp.exp(lg * ((col + 1.0) if forward else (CHUNK - col)))
        chunk_decay = jnp.exp(lg * CHUNK)
        qk_cols = slice(hd * RET_DK, (hd + 1) * RET_DK)
        v_cols = slice(hd * RET_DV, (hd + 1) * RET_DV)
        for c in (range(nchunk) if forward else reversed(range(nchunk))):
            rows = slice(c * CHUNK, (c + 1) * CHUNK)
            q = q_ref[rows, qk_cols]
            k = k_ref[rows, qk_cols]
            v = v_ref[rows, v_cols]
            sc = lax.dot_general(q, k, (((1,), (1,)), ((), ())), preferred_element_type=F32) * decay
            st = state_ref[hd]
            lhs = jnp.concatenate([sc.astype(BF16), (q.astype(F32) * xi).astype(BF16)], axis=1)
            y = jnp.dot(lhs, jnp.concatenate([v, st.astype(BF16)], axis=0), preferred_element_type=F32)
            vz = (v.astype(F32) * zeta).astype(BF16)
            upd = lax.dot_general(k, vz, (((0,), (0,)), ((), ())), preferred_element_type=F32)
            state_ref[hd] = st * chunk_decay + upd
            if forward:
                y = y + yb_ref[rows, v_cols]
                mu = jnp.mean(y, axis=-1, keepdims=True)
                yc = y - mu
                var = jnp.mean(yc * yc, axis=-1, keepdims=True)
                yn = yc * lax.rsqrt(var + LN_EPS)
                o_ref[rows, v_cols] = (g_ref[rows, v_cols].astype(F32) * yn).astype(BF16)
            else:
                o_ref[rows, v_cols] = y


def _retention_pass(lg, rq, rk, rv, extra, batch, seq, *, rb, forward):
    t = rq.shape[0]
    ns = seq // rb
    if forward:
        blk = lambda b, s, lg_ref: (b * ns + s, 0)
    else:
        blk = lambda b, s, lg_ref: (b * ns + ns - 1 - s, 0)
    grid_spec = pltpu.PrefetchScalarGridSpec(
        num_scalar_prefetch=1,
        grid=(batch, ns),
        in_specs=[pl.BlockSpec((rb, RET_QK_W), blk), pl.BlockSpec((rb, RET_QK_W), blk),
                  pl.BlockSpec((rb, RET_V_W), blk)] + [pl.BlockSpec((rb, RET_V_W), blk)] * len(extra),
        out_specs=pl.BlockSpec((rb, RET_V_W), blk),
        scratch_shapes=[pltpu.VMEM((RET_HEADS, RET_DK, RET_DV), F32)],
    )
    return pl.pallas_call(
        functools.partial(_retention_kernel, rb=rb, forward=forward),
        grid_spec=grid_spec,
        out_shape=jax.ShapeDtypeStruct((t, RET_V_W), BF16 if forward else F32),
        compiler_params=pltpu.CompilerParams(dimension_semantics=("arbitrary", "arbitrary"),
                                             vmem_limit_bytes=VMEM_LIMIT_BYTES),
        name="retention_fwd" if forward else "retention_bwd",
    )(lg, rq, rk, rv, *extra)


def _retention(rq, rk, rv, rg, lg, batch, seq, *, rb=512):
    y_bwd = _retention_pass(lg, rq, rk, rv, (), batch, seq, rb=rb, forward=False)
    return _retention_pass(lg, rq, rk, rv, (rg, y_bwd), batch, seq, rb=rb, forward=True)


def _attention_kernel(q_ref, k_ref, vt_ref, o_ref, m_ref, l_ref, acc_ref, *, bounded):
    ki = pl.program_id(3)

    @pl.when(ki == 0)
    def _():
        if not bounded:
            m_ref[...] = jnp.full_like(m_ref, NEG_INF)
        l_ref[...] = jnp.zeros_like(l_ref)
        acc_ref[...] = jnp.zeros_like(acc_ref)

    k = k_ref[...]
    vt = vt_ref[...]
    def scores(g):
        q = q_ref[:, g * ATT_DH:(g + 1) * ATT_DH]
        return lax.dot_general(k, q, (((1,), (1,)), ((), ())), preferred_element_type=F32)

    st_next = scores(0)
    for g in range(ATT_GROUP):
        st = st_next
        if g + 1 < ATT_GROUP:
            st_next = scores(g + 1)
        if bounded:
            pt = jnp.exp2(st)
            l_ref[g:g + 1, :] += jnp.sum(pt, axis=0, keepdims=True)
            acc_ref[g] += jnp.dot(vt, pt.astype(BF16), preferred_element_type=F32)
            continue
        m_prev = m_ref[g:g + 1, :]
        m_new = jnp.maximum(m_prev, jnp.max(st, axis=0, keepdims=True))
        alpha = jnp.exp2(m_prev - m_new)
        pt = jnp.exp2(st - m_new)
        l_ref[g:g + 1, :] = alpha * l_ref[g:g + 1, :] + jnp.sum(pt, axis=0, keepdims=True)
        acc_ref[g] = alpha * acc_ref[g] + jnp.dot(vt, pt.astype(BF16), preferred_element_type=F32)
        m_ref[g:g + 1, :] = m_new

    @pl.when(ki == pl.num_programs(3) - 1)
    def _():
        for g in range(ATT_GROUP):
            out = acc_ref[g] / l_ref[g:g + 1, :]
            o_ref[:, g * ATT_DH:(g + 1) * ATT_DH] = out.T.astype(BF16)


def _attention(aq, ak, avt, score_bound, batch, seq):
    return lax.cond(score_bound <= ATT_LOGIT_LIMIT,
                    functools.partial(_attention_call, batch=batch, seq=seq, bounded=True),
                    functools.partial(_attention_call, batch=batch, seq=seq, bounded=False),
                    aq, ak, avt)


def _attention_call(aq, ak, avt, *, batch, seq, bounded, tq=512, tk=4096):
    t = aq.shape[0]
    nq = seq // tq
    nk = seq // tk
    gw = ATT_GROUP * ATT_DH
    return pl.pallas_call(
        functools.partial(_attention_kernel, bounded=bounded),
        grid=(batch, ATT_KV_HEADS, nq, nk),
        in_specs=[
            pl.BlockSpec((tq, gw), lambda b, g, qi, ki: (b * nq + qi, g)),
            pl.BlockSpec((tk, ATT_DH), lambda b, g, qi, ki: (b * nk + ki, g)),
            pl.BlockSpec((ATT_DH, tk), lambda b, g, qi, ki: (g, b * nk + ki)),
        ],
        out_specs=pl.BlockSpec((tq, gw), lambda b, g, qi, ki: (b * nq + qi, g)),
        out_shape=jax.ShapeDtypeStruct((t, ATT_Q_W), BF16),
        scratch_shapes=[
            pltpu.VMEM((SUBLANES, tq), F32),
            pltpu.VMEM((SUBLANES, tq), F32),
            pltpu.VMEM((ATT_GROUP, ATT_DH, tq), F32),
        ],
        compiler_params=pltpu.CompilerParams(
            dimension_semantics=("arbitrary", "arbitrary", "arbitrary", "arbitrary"),
            vmem_limit_bytes=VMEM_LIMIT_BYTES),
        name="attention_bounded" if bounded else "attention",
    )(aq, ak, avt)


def _merge_kernel(x_ref, yr_ref, ya_ref, ys_ref, gates_ref, wb_ref, wo_ref, g_ref, b_ref, o_ref, *, nsplit):
    sub = x_ref.shape[0] // nsplit

    def branches(r):
        rows = slice(r * sub, (r + 1) * sub)
        merged = None
        for i, y_ref in enumerate((yr_ref, ya_ref, ys_ref)):
            gate = gates_ref[rows, i * D_MODEL:(i + 1) * D_MODEL].astype(F32)
            term = gate * jnp.dot(y_ref[rows, :], wb_ref[i], preferred_element_type=F32)
            merged = term if merged is None else merged + term
        return merged.astype(BF16)

    nxt = branches(0)
    for r in range(nsplit):
        rows = slice(r * sub, (r + 1) * sub)
        cur = nxt
        if r + 1 < nsplit:
            nxt = branches(r + 1)
        proj = jnp.dot(cur, wo_ref[...], preferred_element_type=F32)
        o_ref[rows, :] = _layer_norm_rows(DEEPNORM_ALPHA * x_ref[rows, :] + proj, g_ref[...], b_ref[...])


def _merge(x2d, y_ret, y_att, y_sgu, gates, wb_bf, wo_bf, ln_g, ln_b, *, tb=1024, nsplit=4):
    t = x2d.shape[0]
    row = lambda i: (i, 0)
    const = lambda i: (0, 0)
    return pl.pallas_call(
        functools.partial(_merge_kernel, nsplit=nsplit),
        grid=(t // tb,),
        in_specs=[
            pl.BlockSpec((tb, D_MODEL), row),
            pl.BlockSpec((tb, D_MODEL), row),
            pl.BlockSpec((tb, D_MODEL), row),
            pl.BlockSpec((tb, D_MODEL), row),
            pl.BlockSpec((tb, GATE_W), row),
            pl.BlockSpec((N_BRANCH, D_MODEL, D_MODEL), lambda i: (0, 0, 0)),
            pl.BlockSpec((D_MODEL, D_MODEL), const),
            pl.BlockSpec((1, D_MODEL), const),
            pl.BlockSpec((1, D_MODEL), const),
        ],
        out_specs=pl.BlockSpec((tb, D_MODEL), row),
        out_shape=jax.ShapeDtypeStruct((t, D_MODEL), F32),
        compiler_params=pltpu.CompilerParams(dimension_semantics=("arbitrary",),
                                             vmem_limit_bytes=VMEM_LIMIT_BYTES),
        name="merge",
    )(x2d, y_ret, y_att, y_sgu, gates, wb_bf, wo_bf, ln_g, ln_b)


PEER_PAD_ID = 1024.0


def _peer_candidate_ids():
    pairs = [(0, b) for b in range(PEER_TOPK)]
    pairs += [(a, b) for a in range(1, SUBLANES) for b in range(SUBLANES)]
    pairs += [(a, 0) for a in range(SUBLANES, PEER_TOPK)]
    ids = [float(a * PEER_TOPK + b) if (a + 1) * (b + 1) <= PEER_TOPK else PEER_PAD_ID for a, b in pairs]
    return np.asarray(ids, np.float32)


def _topk_rows(v, ids, k, group):
    nblk = v.shape[0] // SUBLANES
    lists = []
    for g0 in range(0, nblk, group):
        lv = [v[b * SUBLANES:(b + 1) * SUBLANES] for b in range(g0, g0 + group)]
        li = [ids[b * SUBLANES:(b + 1) * SUBLANES] for b in range(g0, g0 + group)]
        for rnd in range(group):
            for a in range(rnd % 2, group - 1, 2):
                swap = lv[a + 1] > lv[a]
                lv[a], lv[a + 1] = jnp.where(swap, lv[a + 1], lv[a]), jnp.where(swap, lv[a], lv[a + 1])
                li[a], li[a + 1] = jnp.where(swap, li[a + 1], li[a]), jnp.where(swap, li[a], li[a + 1])
        lists.append((lv, li))
    vals, idxs = [], []
    for _ in range(k):
        head = functools.reduce(jnp.maximum, [lv[0] for lv, _ in lists])
        m = jnp.max(head, axis=0, keepdims=True)
        cand = functools.reduce(jnp.minimum,
                                [jnp.where(lv[0] == m, li[0], 2.0 * PEER_PAD_ID) for lv, li in lists])
        idx = jnp.min(cand, axis=0, keepdims=True)
        vals.append(m)
        idxs.append(idx)
        for lv, li in lists:
            won = li[0] == idx
            for j in range(group - 1):
                lv[j] = jnp.where(won, lv[j + 1], lv[j])
                li[j] = jnp.where(won, li[j + 1], li[j])
            lv[group - 1] = jnp.where(won, NEG_INF, lv[group - 1])
    return vals, idxs


def _route_kernel(x_ref, wq_ref, keys_ref, cid_ref, i1_ref, i2_ref, gate_ref):
    tb = x_ref.shape[0]
    q = jnp.dot(x_ref[...].astype(BF16), wq_ref[...], preferred_element_type=F32).astype(BF16)
    half = PEER_DQ // 2
    key_row = lax.broadcasted_iota(jnp.int32, (PEER_NKEYS, tb), 0).astype(F32)
    arow = lax.broadcasted_iota(jnp.int32, (PEER_TOPK, tb), 0).astype(F32)
    cand_ids = cid_ref[...]
    i1_rows, i2_rows, gate_rows = [], [], []
    for h in range(PEER_HEADS):
        sub = []
        for p in range(2):
            qs = q[:, (h * 2 + p) * half:(h * 2 + p + 1) * half]
            sc = lax.dot_general(keys_ref[h * 2 + p], qs, (((1,), (1,)), ((), ())),
                                 preferred_element_type=F32)
            sub.append(_topk_rows(sc, key_row, PEER_TOPK, group=8))
        (s1, i1), (s2, i2) = sub
        s1m = jnp.concatenate(s1, axis=0)
        s2m = jnp.concatenate(s2, axis=0)
        i1m = jnp.concatenate(i1, axis=0)
        i2m = jnp.concatenate(i2, axis=0)
        pieces = [s1[0] + s2m[:SUBLANES], s1[0] + s2m[SUBLANES:]]
        pieces += [s1[a] + s2m[:SUBLANES] for a in range(1, SUBLANES)]
        pieces += [s1m[SUBLANES:] + s2[0]]
        cand = jnp.where(cand_ids < PEER_PAD_ID, jnp.concatenate(pieces, axis=0), NEG_INF)
        top_s, top_pos = _topk_rows(cand, cand_ids, PEER_TOPK, group=5)
        top_s = jnp.concatenate(top_s, axis=0)
        e = jnp.exp(top_s - top_s[0:1, :])
        gate_rows.append(e / jnp.sum(e, axis=0, keepdims=True))
        for pos in top_pos:
            a = jnp.floor(pos * (1.0 / PEER_TOPK))
            b = pos - a * PEER_TOPK
            i1_rows.append(jnp.sum(jnp.where(arow == a, i1m, 0.0), axis=0, keepdims=True))
            i2_rows.append(jnp.sum(jnp.where(arow == b, i2m, 0.0), axis=0, keepdims=True))
    i1_ref[...] = jnp.concatenate(i1_rows, axis=0).T
    i2_ref[...] = jnp.concatenate(i2_rows, axis=0).T
    gate_ref[...] = jnp.concatenate(gate_rows, axis=0).T


def _route(x1, wq_bf, keys_bf, *, tb=128):
    t = x1.shape[0]
    hk = PEER_HEADS * PEER_TOPK
    row = lambda i: (i, 0)
    ids = _peer_candidate_ids()
    cand_ids = jnp.asarray(np.ascontiguousarray(np.broadcast_to(ids[:, None], (ids.shape[0], tb))))
    return pl.pallas_call(
        _route_kernel,
        grid=(t // tb,),
        in_specs=[
            pl.BlockSpec((tb, D_MODEL), row),
            pl.BlockSpec((D_MODEL, PEER_HEADS * PEER_DQ), lambda i: (0, 0)),
            pl.BlockSpec((PEER_HEADS * 2, PEER_NKEYS, PEER_DQ // 2), lambda i: (0, 0, 0)),
            pl.BlockSpec(cand_ids.shape, lambda i: (0, 0)),
        ],
        out_specs=[pl.BlockSpec((tb, hk), row)] * 3,
        out_shape=[jax.ShapeDtypeStruct((t, hk), F32)] * 3,
        compiler_params=pltpu.CompilerParams(dimension_semantics=("arbitrary",),
                                             vmem_limit_bytes=VMEM_LIMIT_BYTES),
        name="peer_route",
    )(x1, wq_bf, keys_bf, cand_ids)


def _fp8_scale(bound):
    return jnp.exp2(jnp.floor(jnp.log2(FP8_TARGET / jnp.maximum(bound, FP8_TINY))))


def _peer_kernel(x_ref, i1_ref, i2_ref, gate_ref, ut_ref, stat_ref, v_ref, g_ref, b_ref, o_ref,
                 gs_ref, x8_ref, hscale_ref, wscale_ref, acc_ref, *, tb, ec, pitch, unroll):
    e = pl.program_id(1)
    ncl = ec // PEER_NKEYS

    @pl.when(e == 0)
    def _():
        x = x_ref[...]
        sx = _fp8_scale(jnp.max(jnp.max(jnp.abs(x), axis=0, keepdims=True), axis=1, keepdims=True))
        x8_ref[...] = (x * sx).astype(FP8)
        hscale_ref[...] = jnp.broadcast_to(stat_ref[0:1, 0:1] / sx, hscale_ref.shape)
        row_norm = jnp.sqrt(jnp.sum(x * x, axis=1, keepdims=True))
        sw = _fp8_scale(PEER_HEADS * row_norm * stat_ref[0:1, 1:2])
        wscale_ref[...] = jnp.broadcast_to(sw, wscale_ref.shape)
        acc_ref[...] = jnp.zeros_like(acc_ref)
        key_id = lax.broadcasted_iota(jnp.int32, (PEER_NKEYS, PEER_NKEYS), 0).astype(F32)

        def build(blk, carry):
            for r in range(unroll):
                t = blk * unroll + r
                i1 = jnp.broadcast_to(i1_ref[pl.ds(t, 1), :], (PEER_NKEYS, PEER_NKEYS))
                i2 = jnp.broadcast_to(i2_ref[pl.ds(t, 1), :], (PEER_NKEYS, PEER_NKEYS))
                gt = jnp.broadcast_to(0.5 * gate_ref[pl.ds(t, 1), :] * wscale_ref[pl.ds(t, 1), :]
                                      * hscale_ref[0:1, :], (PEER_NKEYS, PEER_NKEYS))
                a_t = jnp.where(i1 == key_id, gt, 0.0).astype(BF16)
                b_t = jnp.where(i2 == key_id, 1.0, 0.0).astype(BF16)
                g_tok = lax.dot_general(a_t, b_t, (((1,), (1,)), ((), ())), preferred_element_type=F32)
                g_rnd = g_tok.astype(BF16).astype(F32).reshape(PEER_NKEYS // PEER_PAIR, PEER_PAIR, PEER_NKEYS)
                hi = lax.bitcast_convert_type(g_rnd[:, :SUBLANES, :].reshape(PEER_PLANES, PEER_NKEYS), jnp.uint32)
                lo = lax.bitcast_convert_type(g_rnd[:, SUBLANES:, :].reshape(PEER_PLANES, PEER_NKEYS),
                                              jnp.uint32) >> 16
                gs_ref[pl.ds(t, PEER_PLANES, stride=pitch), :] = hi | lo
            return carry

        lax.fori_loop(0, tb // unroll, build, 0)

    d = jnp.dot(x8_ref[...], ut_ref[...], preferred_element_type=F32)
    act = d * (1.0 + lax.erf(d * (hscale_ref[0:1, 0:1] * 2.0 ** -0.5)))
    parts = [None] * ncl
    for grp in range(ncl // PEER_PAIR):
        for i in range(SUBLANES):
            plane = (e * (ncl // PEER_PAIR) + grp) * SUBLANES + i
            word = gs_ref[pl.ds(pl.multiple_of(plane * pitch, SUBLANES), tb), :]
            halves = (lax.bitcast_convert_type(word & jnp.uint32(0xFFFF0000), F32),
                      lax.bitcast_convert_type(word << 16, F32))
            for half, gval in enumerate(halves):
                cl = grp * PEER_PAIR + half * SUBLANES + i
                parts[cl] = act[:, cl * PEER_NKEYS:(cl + 1) * PEER_NKEYS] * gval
    w = jnp.concatenate(parts, axis=1).astype(FP8)
    acc_ref[...] += jnp.dot(w, v_ref[...], preferred_element_type=F32)

    @pl.when(e == pl.num_programs(1) - 1)
    def _():
        peer = acc_ref[...] * (stat_ref[0:1, 2:3] / wscale_ref[:, 0:1])
        o_ref[...] = _layer_norm_rows(DEEPNORM_ALPHA * x_ref[...] + peer, g_ref[...], b_ref[...])


def _peer(x1, i1, i2, gate, ut_f8, stats, v_f8, ln_g, ln_b, *, tb=512, ec=2048, unroll=128):
    t = x1.shape[0]
    n_exp = v_f8.shape[0]
    hk = PEER_HEADS * PEER_TOPK
    pitch = tb + SUBLANES
    row = lambda i, e: (i, 0)
    const = lambda i, e: (0, 0)
    return pl.pallas_call(
        functools.partial(_peer_kernel, tb=tb, ec=ec, pitch=pitch, unroll=unroll),
        grid=(t // tb, n_exp // ec),
        in_specs=[
            pl.BlockSpec((tb, D_MODEL), row),
            pl.BlockSpec((tb, hk), row),
            pl.BlockSpec((tb, hk), row),
            pl.BlockSpec((tb, hk), row),
            pl.BlockSpec((D_MODEL, ec), lambda i, e: (0, e)),
            pl.BlockSpec((1, LANES), const),
            pl.BlockSpec((ec, D_MODEL), lambda i, e: (e, 0)),
            pl.BlockSpec((1, D_MODEL), const),
            pl.BlockSpec((1, D_MODEL), const),
        ],
        out_specs=pl.BlockSpec((tb, D_MODEL), row),
        out_shape=jax.ShapeDtypeStruct((t, D_MODEL), F32),
        scratch_shapes=[
            pltpu.VMEM((PEER_PLANES * pitch, PEER_NKEYS), jnp.uint32),
            pltpu.VMEM((tb, D_MODEL), FP8),
            pltpu.VMEM((SUBLANES, LANES), F32),
            pltpu.VMEM((tb, LANES), F32),
            pltpu.VMEM((tb, D_MODEL), F32),
        ],
        compiler_params=pltpu.CompilerParams(dimension_semantics=("arbitrary", "arbitrary"),
                                             vmem_limit_bytes=VMEM_LIMIT_BYTES),
        name="peer_dense",
    )(x1, i1, i2, gate, ut_f8, stats, v_f8, ln_g, ln_b)


def _rope_tables(seq):
    rows = seq // GRID_W
    row = jnp.repeat(jnp.arange(rows, dtype=F32), GRID_W)
    col = jnp.tile(jnp.arange(GRID_W, dtype=F32), rows)
    n_freq = ATT_DH // 4
    inv_freq = ROPE_THETA ** (-jnp.arange(n_freq, dtype=F32) / n_freq)
    ang = jnp.concatenate([row[:, None] * inv_freq, col[:, None] * inv_freq], axis=-1)
    cos, sin = jnp.cos(ang), jnp.sin(ang)
    return jnp.concatenate([cos, cos], axis=-1), jnp.concatenate([-sin, sin], axis=-1)


def _layer(x2d, batch, seq, cos2, sin2, p):
    rq, rk, rv, rg, aq, ak, av, y_sgu, gates = _inproj(
        x2d, seq, p["w_in"], cos2, sin2, p["b_gate"], p["qn"], p["kn"], p["sgu_ln_g"], p["sgu_ln_b"],
        p["sgu_w"], p["sgu_bt"])
    y_ret = _retention(rq, rk, rv, rg, p["lg"], batch, seq)
    y_att = _attention(aq, ak, jnp.transpose(av), p["att_score_bound"], batch, seq)
    x1 = _merge(x2d, y_ret, y_att, y_sgu, gates, p["w_branch"], p["w_out"], p["ln1_g"], p["ln1_b"])
    i1, i2, gate = _route(x1, p["peer_wq"], p["peer_keys"])
    return _peer(x1, i1, i2, gate, p["peer_ut"], p["peer_stats"], p["peer_v"], p["ln2_g"], p["ln2_b"])


def _prep_layer(l, w_in, b_gate, ret_decay_fwd, ret_decay_bwd, attn_q_norm, attn_k_norm,
                sgu_ln_g, sgu_ln_b, sgu_w, sgu_b, w_branch, w_out, ln1_g, ln1_b,
                peer_wq, peer_keys, peer_u, peer_v, ln2_g, ln2_b):
    half = PEER_DQ // 2
    u_norm = jnp.sqrt(jnp.max(jnp.sum(jnp.square(peer_u[l]), axis=1)))
    u_scale = _fp8_scale(u_norm)
    v_scale = _fp8_scale(jnp.max(jnp.abs(peer_v[l])))
    stats = jnp.zeros((1, LANES), F32).at[0, 0].set(1.0 / u_scale).at[0, 1].set(u_norm).at[0, 2].set(1.0 / v_scale)
    return dict(
        w_in=w_in[l].astype(BF16),
        b_gate=b_gate[l].reshape(1, GATE_W),
        lg=jnp.stack([jax.nn.log_sigmoid(ret_decay_fwd[l].astype(F32)),
                      jax.nn.log_sigmoid(ret_decay_bwd[l].astype(F32))]),
        att_score_bound=(ATT_DH * ATT_Q_SCALE * ATT_BOUND_SLACK
                         * jnp.max(jnp.abs(attn_q_norm[l])) * jnp.max(jnp.abs(attn_k_norm[l]))),
        qn=attn_q_norm[l].reshape(1, ATT_DH),
        kn=attn_k_norm[l].reshape(1, ATT_DH),
        sgu_ln_g=sgu_ln_g[l].reshape(1, SGU_WIDTH),
        sgu_ln_b=sgu_ln_b[l].reshape(1, SGU_WIDTH),
        sgu_w=sgu_w[l].astype(BF16),
        sgu_bt=jnp.transpose(sgu_b[l]),
        w_branch=w_branch[l].astype(BF16),
        w_out=w_out[l].astype(BF16),
        ln1_g=ln1_g[l].reshape(1, D_MODEL),
        ln1_b=ln1_b[l].reshape(1, D_MODEL),
        peer_wq=peer_wq[l].astype(BF16),
        peer_keys=peer_keys[l].reshape(PEER_HEADS * 2, PEER_NKEYS, half).astype(BF16),
        peer_ut=jnp.transpose(peer_u[l] * u_scale).astype(FP8),
        peer_stats=stats,
        peer_v=(peer_v[l] * v_scale).astype(FP8),
        ln2_g=ln2_g[l].reshape(1, D_MODEL),
        ln2_b=ln2_b[l].reshape(1, D_MODEL),
    )


def kernel(x_prompt, x_sample, w_in, b_gate, ret_decay_fwd, ret_decay_bwd, attn_q_norm, attn_k_norm,
           sgu_ln_g, sgu_ln_b, sgu_w, sgu_b, w_branch, w_out, ln1_g, ln1_b,
           peer_wq, peer_keys, peer_u, peer_v, ln2_g, ln2_b):
    weights = (w_in, b_gate, ret_decay_fwd, ret_decay_bwd, attn_q_norm, attn_k_norm,
               sgu_ln_g, sgu_ln_b, sgu_w, sgu_b, w_branch, w_out, ln1_g, ln1_b,
               peer_wq, peer_keys, peer_u, peer_v, ln2_g, ln2_b)
    layers = [_prep_layer(l, *weights) for l in range(DEPTH)]
    outs = []
    for x in (x_prompt, x_sample):
        batch, seq, _ = x.shape
        cos2, sin2 = _rope_tables(seq)
        h = x.reshape(batch * seq, D_MODEL)
        for p in layers:
            h = _layer(h, batch, seq, cos2, sin2, p)
        outs.append(h.reshape(batch, seq, D_MODEL))
    return tuple(outs)
```

```python
import functools
import math

import jax
import jax.numpy as jnp
import numpy as np
from jax import lax
from jax.experimental import pallas as pl
from jax.experimental.pallas import tpu as pltpu

F32 = jnp.float32
BF16 = jnp.bfloat16
FP8 = jnp.float8_e4m3fn

D_MODEL = 1024
DEPTH = 2
GRID_W = 64
CHUNK = 128
RET_HEADS = 4
RET_DK = 128
RET_DV = 256
ATT_HEADS = 8
ATT_KV_HEADS = 2
ATT_DH = 128
ATT_GROUP = ATT_HEADS // ATT_KV_HEADS
SGU_GROUPS = 4
SGU_WIDTH = 1024
SGU_GW = SGU_WIDTH // SGU_GROUPS
PEER_HEADS = 8
PEER_NKEYS = 128
PEER_DQ = 256
PEER_TOPK = 16
PEER_PLANES = PEER_NKEYS // 2
N_BRANCH = 3
ROPE_THETA = 10000.0
DEEPNORM_ALPHA = (2 * DEPTH) ** 0.25
LN_EPS = 1e-5
RMS_EPS = 1e-6

RET_QK_W = RET_HEADS * RET_DK
RET_V_W = RET_HEADS * RET_DV
ATT_Q_W = ATT_HEADS * ATT_DH
ATT_KV_W = ATT_KV_HEADS * ATT_DH
GATE_W = N_BRANCH * D_MODEL
SPLITS = (RET_QK_W, RET_QK_W, RET_V_W, RET_V_W, ATT_Q_W, ATT_KV_W, ATT_KV_W, SGU_WIDTH, SGU_WIDTH, GATE_W)
OFFS = tuple(int(sum(SPLITS[:i])) for i in range(len(SPLITS) + 1))
IN_W = OFFS[-1]

LANES = 128
SUBLANES = 8
PEER_PAIR = 2 * SUBLANES
VMEM_LIMIT_BYTES = 56 * 1024 * 1024

NEG_INF = float("-inf")
FP8_TARGET = 224.0
FP8_TINY = 1e-30
ATT_Q_SCALE = ATT_DH ** -0.5 * math.log2(math.e)
ATT_LOGIT_LIMIT = 64.0
ATT_BOUND_SLACK = 1.02


def _gelu(x):
    return 0.5 * x * (1.0 + lax.erf(x * (2.0 ** -0.5)))


def _rope(x, cos2, sin2):
    return x * cos2 + pltpu.roll(x, ATT_DH // 2, 1) * sin2


def _layer_norm_rows(x, g, b):
    mu = jnp.mean(x, axis=-1, keepdims=True)
    xc = x - mu
    var = jnp.mean(xc * xc, axis=-1, keepdims=True)
    return xc * lax.rsqrt(var + LN_EPS) * g + b


def _inproj_kernel(x_ref, w_ref, cos_ref, sin_ref, bgate_ref, qn_ref, kn_ref, lng_ref, lnb_ref,
                   sw_ref, sb_ref,
                   rq_ref, rk_ref, rv_ref, rg_ref, aq_ref, ak_ref, av_ref, ysgu_ref, gates_ref):
    tb = x_ref.shape[0]
    xb = x_ref[...].astype(BF16)
    cos2 = cos_ref[...]
    sin2 = sin_ref[...]

    def proj(i):
        return jnp.dot(xb, w_ref[:, OFFS[i]:OFFS[i + 1]], preferred_element_type=F32)

    gates_ref[...] = jax.nn.sigmoid(proj(9) + bgate_ref[...]).astype(BF16)

    u = _gelu(proj(7))
    vn = _layer_norm_rows(_gelu(proj(8)), lng_ref[...], lnb_ref[...]).astype(BF16)
    sb = sb_ref[...]
    for c in range(tb // CHUNK):
        rows = slice(c * CHUNK, (c + 1) * CHUNK)
        for g in range(SGU_GROUPS):
            cols = slice(g * SGU_GW, (g + 1) * SGU_GW)
            mixed = jnp.dot(sw_ref[g], vn[rows, cols], preferred_element_type=F32) + sb[:, g:g + 1]
            ysgu_ref[rows, cols] = (u[rows, cols] * mixed).astype(BF16)

    h = proj(4)
    qn = qn_ref[...]
    for hd in range(ATT_HEADS):
        sl = slice(hd * ATT_DH, (hd + 1) * ATT_DH)
        t = h[:, sl]
        t = t * lax.rsqrt(jnp.mean(t * t, axis=-1, keepdims=True) + RMS_EPS) * qn
        aq_ref[:, sl] = (_rope(t, cos2, sin2) * ATT_Q_SCALE).astype(BF16)
    h = proj(5)
    kn = kn_ref[...]
    for hd in range(ATT_KV_HEADS):
        sl = slice(hd * ATT_DH, (hd + 1) * ATT_DH)
        t = h[:, sl]
        t = t * lax.rsqrt(jnp.mean(t * t, axis=-1, keepdims=True) + RMS_EPS) * kn
        ak_ref[:, sl] = _rope(t, cos2, sin2).astype(BF16)
    av_ref[...] = proj(6).astype(BF16)

    h = proj(0)
    for hd in range(RET_HEADS):
        sl = slice(hd * RET_DK, (hd + 1) * RET_DK)
        rq_ref[:, sl] = _rope(h[:, sl], cos2, sin2).astype(BF16)
    h = proj(1)
    for hd in range(RET_HEADS):
        sl = slice(hd * RET_DK, (hd + 1) * RET_DK)
        rk_ref[:, sl] = (_rope(h[:, sl], cos2, sin2) * (RET_DK ** -0.5)).astype(BF16)
    rv_ref[...] = proj(2).astype(BF16)
    h = proj(3)
    rg_ref[...] = (h * jax.nn.sigmoid(h)).astype(BF16)


def _inproj(x2d, seq, w_bf, cos2, sin2, b_gate, qn, kn, lng, lnb, sgu_w_bf, sgu_bt, *, tb=256):
    t = x2d.shape[0]
    nblk_seq = seq // tb
    const = lambda i: (0, 0)
    row = lambda i: (i, 0)
    pos = lambda i: (i % nblk_seq, 0)
    outs = [(RET_QK_W, BF16), (RET_QK_W, BF16), (RET_V_W, BF16), (RET_V_W, BF16), (ATT_Q_W, BF16),
            (ATT_KV_W, BF16), (ATT_KV_W, BF16), (SGU_WIDTH, BF16), (GATE_W, BF16)]
    return pl.pallas_call(
        _inproj_kernel,
        grid=(t // tb,),
        in_specs=[
            pl.BlockSpec((tb, D_MODEL), row),
            pl.BlockSpec((D_MODEL, IN_W), const, pipeline_mode=pl.Buffered(1)),
            pl.BlockSpec((tb, ATT_DH), pos),
            pl.BlockSpec((tb, ATT_DH), pos),
            pl.BlockSpec((1, GATE_W), const),
            pl.BlockSpec((1, ATT_DH), const),
            pl.BlockSpec((1, ATT_DH), const),
            pl.BlockSpec((1, SGU_WIDTH), const),
            pl.BlockSpec((1, SGU_WIDTH), const),
            pl.BlockSpec((SGU_GROUPS, CHUNK, CHUNK), lambda i: (0, 0, 0)),
            pl.BlockSpec((CHUNK, SGU_GROUPS), const),
        ],
        out_specs=[pl.BlockSpec((tb, w), row) for w, _ in outs],
        out_shape=[jax.ShapeDtypeStruct((t, w), d) for w, d in outs],
        compiler_params=pltpu.CompilerParams(dimension_semantics=("arbitrary",),
                                             vmem_limit_bytes=VMEM_LIMIT_BYTES),
        name="inproj",
    )(x2d, w_bf, cos2, sin2, b_gate, qn, kn, lng, lnb, sgu_w_bf, sgu_bt)


def _retention_kernel(lg_ref, q_ref, k_ref, v_ref, *rest, rb, forward):
    if forward:
        g_ref, yb_ref, o_ref, state_ref = rest
    else:
        o_ref, state_ref = rest
    nchunk = rb // CHUNK

    @pl.when(pl.program_id(1) == 0)
    def _():
        state_ref[...] = jnp.zeros_like(state_ref)

    n_i = lax.broadcasted_iota(jnp.int32, (CHUNK, CHUNK), 0)
    m_i = lax.broadcasted_iota(jnp.int32, (CHUNK, CHUNK), 1)
    col = lax.broadcasted_iota(jnp.int32, (CHUNK, 1), 0).astype(F32)
    dist = ((n_i - m_i) if forward else (m_i - n_i)).astype(F32)
    live = (n_i >= m_i) if forward else (m_i > n_i)

    for hd in range(RET_HEADS):
        lg = lg_ref[0 if forward else 1, hd]
        decay = jnp.where(live, jnp.exp(lg * jnp.maximum(dist, 0.0)), 0.0)
        zeta = jnp.exp(lg * ((CHUNK - 1.0 - col) if forward else col))
        xi = jnp.exp(lg * ((col + 1.0) if forward else (CHUNK - col)))
        chunk_decay = jnp.exp(lg * CHUNK)
        qk_cols = slice(hd * RET_DK, (hd + 1) * RET_DK)
        v_cols = slice(hd * RET_DV, (hd + 1) * RET_DV)
        order = list(range(nchunk) if forward else reversed(range(nchunk)))
        qs, vs, scs, upds = {}, {}, {}, {}
        for c in order:
            rows = slice(c * CHUNK, (c + 1) * CHUNK)
            q = qs[c] = q_ref[rows, qk_cols]
            k = k_ref[rows, qk_cols]
            v = vs[c] = v_ref[rows, v_cols]
            sc = lax.dot_general(q, k, (((1,), (1,)), ((), ())), preferred_element_type=F32) * decay
            scs[c] = sc.astype(BF16)
            vz = (v.astype(F32) * zeta).astype(BF16)
            upds[c] = lax.dot_general(k, vz, (((0,), (0,)), ((), ())), preferred_element_type=F32)
        for c in order:
            rows = slice(c * CHUNK, (c + 1) * CHUNK)
            q, v = qs[c], vs[c]
            st = state_ref[hd]
            lhs = jnp.concatenate([scs[c], (q.astype(F32) * xi).astype(BF16)], axis=1)
            y = jnp.dot(lhs, jnp.concatenate([v, st.astype(BF16)], axis=0), preferred_element_type=F32)
            state_ref[hd] = st * chunk_decay + upds[c]
            if forward:
                y = y + yb_ref[rows, v_cols]
                mu = jnp.mean(y, axis=-1, keepdims=True)
                yc = y - mu
                var = jnp.mean(yc * yc, axis=-1, keepdims=True)
                yn = yc * lax.rsqrt(var + LN_EPS)
                o_ref[rows, v_cols] = (g_ref[rows, v_cols].astype(F32) * yn).astype(BF16)
            else:
                o_ref[rows, v_cols] = y


def _retention_pass(lg, rq, rk, rv, extra, batch, seq, *, rb, forward):
    t = rq.shape[0]
    ns = seq // rb
    if forward:
        blk = lambda b, s, lg_ref: (b * ns + s, 0)
    else:
        blk = lambda b, s, lg_ref: (b * ns + ns - 1 - s, 0)
    grid_spec = pltpu.PrefetchScalarGridSpec(
        num_scalar_prefetch=1,
        grid=(batch, ns),
        in_specs=[pl.BlockSpec((rb, RET_QK_W), blk), pl.BlockSpec((rb, RET_QK_W), blk),
                  pl.BlockSpec((rb, RET_V_W), blk)] + [pl.BlockSpec((rb, RET_V_W), blk)] * len(extra),
        out_specs=pl.BlockSpec((rb, RET_V_W), blk),
        scratch_shapes=[pltpu.VMEM((RET_HEADS, RET_DK, RET_DV), F32)],
    )
    return pl.pallas_call(
        functools.partial(_retention_kernel, rb=rb, forward=forward),
        grid_spec=grid_spec,
        out_shape=jax.ShapeDtypeStruct((t, RET_V_W), BF16 if forward else F32),
        compiler_params=pltpu.CompilerParams(dimension_semantics=("arbitrary", "arbitrary"),
                                             vmem_limit_bytes=VMEM_LIMIT_BYTES),
        name="retention_fwd" if forward else "retention_bwd",
    )(lg, rq, rk, rv, *extra)


def _retention(rq, rk, rv, rg, lg, batch, seq, *, rb=1024):
    y_bwd = _retention_pass(lg, rq, rk, rv, (), batch, seq, rb=rb, forward=False)
    return _retention_pass(lg, rq, rk, rv, (rg, y_bwd), batch, seq, rb=rb, forward=True)


def _attention_kernel(q_ref, k_ref, vt_ref, o_ref, m_ref, l_ref, acc_ref, *, bounded):
    ki = pl.program_id(3)

    @pl.when(ki == 0)
    def _():
        if not bounded:
            m_ref[...] = jnp.full_like(m_ref, NEG_INF)
        l_ref[...] = jnp.zeros_like(l_ref)
        acc_ref[...] = jnp.zeros_like(acc_ref)

    k = k_ref[...]
    vt = vt_ref[...]
    def scores(g):
        q = q_ref[:, g * ATT_DH:(g + 1) * ATT_DH]
        return lax.dot_general(k, q, (((1,), (1,)), ((), ())), preferred_element_type=F32)

    st_next = scores(0)
    for g in range(ATT_GROUP):
        st = st_next
        if g + 1 < ATT_GROUP:
            st_next = scores(g + 1)
        if bounded:
            pt = jnp.exp2(st)
            l_ref[g:g + 1, :] += jnp.sum(pt, axis=0, keepdims=True)
            acc_ref[g] += jnp.dot(vt, pt.astype(BF16), preferred_element_type=F32)
            continue
        m_prev = m_ref[g:g + 1, :]
        m_new = jnp.maximum(m_prev, jnp.max(st, axis=0, keepdims=True))
        alpha = jnp.exp2(m_prev - m_new)
        pt = jnp.exp2(st - m_new)
        l_ref[g:g + 1, :] = alpha * l_ref[g:g + 1, :] + jnp.sum(pt, axis=0, keepdims=True)
        acc_ref[g] = alpha * acc_ref[g] + jnp.dot(vt, pt.astype(BF16), preferred_element_type=F32)
        m_ref[g:g + 1, :] = m_new

    @pl.when(ki == pl.num_programs(3) - 1)
    def _():
        for g in range(ATT_GROUP):
            out = acc_ref[g] / l_ref[g:g + 1, :]
            o_ref[:, g * ATT_DH:(g + 1) * ATT_DH] = out.T.astype(BF16)


def _attention(aq, ak, avt, score_bound, batch, seq):
    return lax.cond(score_bound <= ATT_LOGIT_LIMIT,
                    functools.partial(_attention_call, batch=batch, seq=seq, bounded=True),
                    functools.partial(_attention_call, batch=batch, seq=seq, bounded=False),
                    aq, ak, avt)


def _attention_call(aq, ak, avt, *, batch, seq, bounded, tq=512, tk=4096):
    t = aq.shape[0]
    nq = seq // tq
    nk = seq // tk
    gw = ATT_GROUP * ATT_DH
    return pl.pallas_call(
        functools.partial(_attention_kernel, bounded=bounded),
        grid=(batch, ATT_KV_HEADS, nq, nk),
        in_specs=[
            pl.BlockSpec((tq, gw), lambda b, g, qi, ki: (b * nq + qi, g)),
            pl.BlockSpec((tk, ATT_DH), lambda b, g, qi, ki: (b * nk + ki, g)),
            pl.BlockSpec((ATT_DH, tk), lambda b, g, qi, ki: (g, b * nk + ki)),
        ],
        out_specs=pl.BlockSpec((tq, gw), lambda b, g, qi, ki: (b * nq + qi, g)),
        out_shape=jax.ShapeDtypeStruct((t, ATT_Q_W), BF16),
        scratch_shapes=[
            pltpu.VMEM((SUBLANES, tq), F32),
            pltpu.VMEM((SUBLANES, tq), F32),
            pltpu.VMEM((ATT_GROUP, ATT_DH, tq), F32),
        ],
        compiler_params=pltpu.CompilerParams(
            dimension_semantics=("arbitrary", "arbitrary", "arbitrary", "arbitrary"),
            vmem_limit_bytes=VMEM_LIMIT_BYTES),
        name="attention_bounded" if bounded else "attention",
    )(aq, ak, avt)


def _merge_kernel(x_ref, yr_ref, ya_ref, ys_ref, gates_ref, wb_ref, wo_ref, g_ref, b_ref, o_ref, *, nsplit):
    sub = x_ref.shape[0] // nsplit

    def branches(r):
        rows = slice(r * sub, (r + 1) * sub)
        merged = None
        for i, y_ref in enumerate((yr_ref, ya_ref, ys_ref)):
            gate = gates_ref[rows, i * D_MODEL:(i + 1) * D_MODEL].astype(F32)
            term = gate * jnp.dot(y_ref[rows, :], wb_ref[i], preferred_element_type=F32)
            merged = term if merged is None else merged + term
        return merged.astype(BF16)

    nxt = branches(0)
    for r in range(nsplit):
        rows = slice(r * sub, (r + 1) * sub)
        cur = nxt
        if r + 1 < nsplit:
            nxt = branches(r + 1)
        proj = jnp.dot(cur, wo_ref[...], preferred_element_type=F32)
        o_ref[rows, :] = _layer_norm_rows(DEEPNORM_ALPHA * x_ref[rows, :] + proj, g_ref[...], b_ref[...])


def _merge(x2d, y_ret, y_att, y_sgu, gates, wb_bf, wo_bf, ln_g, ln_b, *, tb=1024, nsplit=4):
    t = x2d.shape[0]
    row = lambda i: (i, 0)
    const = lambda i: (0, 0)
    return pl.pallas_call(
        functools.partial(_merge_kernel, nsplit=nsplit),
        grid=(t // tb,),
        in_specs=[
            pl.BlockSpec((tb, D_MODEL), row),
            pl.BlockSpec((tb, D_MODEL), row),
            pl.BlockSpec((tb, D_MODEL), row),
            pl.BlockSpec((tb, D_MODEL), row),
            pl.BlockSpec((tb, GATE_W), row),
            pl.BlockSpec((N_BRANCH, D_MODEL, D_MODEL), lambda i: (0, 0, 0)),
            pl.BlockSpec((D_MODEL, D_MODEL), const),
            pl.BlockSpec((1, D_MODEL), const),
            pl.BlockSpec((1, D_MODEL), const),
        ],
        out_specs=pl.BlockSpec((tb, D_MODEL), row),
        out_shape=jax.ShapeDtypeStruct((t, D_MODEL), F32),
        compiler_params=pltpu.CompilerParams(dimension_semantics=("arbitrary",),
                                             vmem_limit_bytes=VMEM_LIMIT_BYTES),
        name="merge",
    )(x2d, y_ret, y_att, y_sgu, gates, wb_bf, wo_bf, ln_g, ln_b)


PEER_PAD_ID = 1024.0


def _peer_candidate_ids():
    pairs = [(0, b) for b in range(PEER_TOPK)]
    pairs += [(a, b) for a in range(1, SUBLANES) for b in range(SUBLANES)]
    pairs += [(a, 0) for a in range(SUBLANES, PEER_TOPK)]
    ids = [float(a * PEER_TOPK + b) if (a + 1) * (b + 1) <= PEER_TOPK else PEER_PAD_ID for a, b in pairs]
    return np.asarray(ids, np.float32)


def _topk_rows(v, ids, k, group):
    nblk = v.shape[0] // SUBLANES
    lists = []
    for g0 in range(0, nblk, group):
        lv = [v[b * SUBLANES:(b + 1) * SUBLANES] for b in range(g0, g0 + group)]
        li = [ids[b * SUBLANES:(b + 1) * SUBLANES] for b in range(g0, g0 + group)]
        for rnd in range(group):
            for a in range(rnd % 2, group - 1, 2):
                swap = lv[a + 1] > lv[a]
                lv[a], lv[a + 1] = jnp.where(swap, lv[a + 1], lv[a]), jnp.where(swap, lv[a], lv[a + 1])
                li[a], li[a + 1] = jnp.where(swap, li[a + 1], li[a]), jnp.where(swap, li[a], li[a + 1])
        lists.append((lv, li))
    vals, idxs = [], []
    for _ in range(k):
        head = functools.reduce(jnp.maximum, [lv[0] for lv, _ in lists])
        m = jnp.max(head, axis=0, keepdims=True)
        cand = functools.reduce(jnp.minimum,
                                [jnp.where(lv[0] == m, li[0], 2.0 * PEER_PAD_ID) for lv, li in lists])
        idx = jnp.min(cand, axis=0, keepdims=True)
        vals.append(m)
        idxs.append(idx)
        for lv, li in lists:
            won = li[0] == idx
            for j in range(group - 1):
                lv[j] = jnp.where(won, lv[j + 1], lv[j])
                li[j] = jnp.where(won, li[j + 1], li[j])
            lv[group - 1] = jnp.where(won, NEG_INF, lv[group - 1])
    return vals, idxs


def _route_kernel(x_ref, wq_ref, keys_ref, cid_ref, i1_ref, i2_ref, gate_ref):
    tb = x_ref.shape[0]
    q = jnp.dot(x_ref[...].astype(BF16), wq_ref[...], preferred_element_type=F32).astype(BF16)
    half = PEER_DQ // 2
    key_row = lax.broadcasted_iota(jnp.int32, (PEER_NKEYS, tb), 0).astype(F32)
    arow = lax.broadcasted_iota(jnp.int32, (PEER_TOPK, tb), 0).astype(F32)
    cand_ids = cid_ref[...]
    i1_rows, i2_rows, gate_rows = [], [], []
    for h in range(PEER_HEADS):
        sub = []
        for p in range(2):
            qs = q[:, (h * 2 + p) * half:(h * 2 + p + 1) * half]
            sc = lax.dot_general(keys_ref[h * 2 + p], qs, (((1,), (1,)), ((), ())),
                                 preferred_element_type=F32)
            sub.append(_topk_rows(sc, key_row, PEER_TOPK, group=8))
        (s1, i1), (s2, i2) = sub
        s1m = jnp.concatenate(s1, axis=0)
        s2m = jnp.concatenate(s2, axis=0)
        i1m = jnp.concatenate(i1, axis=0)
        i2m = jnp.concatenate(i2, axis=0)
        pieces = [s1[0] + s2m[:SUBLANES], s1[0] + s2m[SUBLANES:]]
        pieces += [s1[a] + s2m[:SUBLANES] for a in range(1, SUBLANES)]
        pieces += [s1m[SUBLANES:] + s2[0]]
        cand = jnp.where(cand_ids < PEER_PAD_ID, jnp.concatenate(pieces, axis=0), NEG_INF)
        top_s, top_pos = _topk_rows(cand, cand_ids, PEER_TOPK, group=5)
        top_s = jnp.concatenate(top_s, axis=0)
        e = jnp.exp(top_s - top_s[0:1, :])
        gate_rows.append(e / jnp.sum(e, axis=0, keepdims=True))
        for pos in top_pos:
            a = jnp.floor(pos * (1.0 / PEER_TOPK))
            b = pos - a * PEER_TOPK
            i1_rows.append(jnp.sum(jnp.where(arow == a, i1m, 0.0), axis=0, keepdims=True))
            i2_rows.append(jnp.sum(jnp.where(arow == b, i2m, 0.0), axis=0, keepdims=True))
    i1_ref[...] = jnp.concatenate(i1_rows, axis=0).T
    i2_ref[...] = jnp.concatenate(i2_rows, axis=0).T
    gate_ref[...] = jnp.concatenate(gate_rows, axis=0).T


def _route(x1, wq_bf, keys_bf, *, tb=128):
    t = x1.shape[0]
    hk = PEER_HEADS * PEER_TOPK
    row = lambda i: (i, 0)
    ids = _peer_candidate_ids()
    cand_ids = jnp.asarray(np.ascontiguousarray(np.broadcast_to(ids[:, None], (ids.shape[0], tb))))
    return pl.pallas_call(
        _route_kernel,
        grid=(t // tb,),
        in_specs=[
            pl.BlockSpec((tb, D_MODEL), row),
            pl.BlockSpec((D_MODEL, PEER_HEADS * PEER_DQ), lambda i: (0, 0)),
            pl.BlockSpec((PEER_HEADS * 2, PEER_NKEYS, PEER_DQ // 2), lambda i: (0, 0, 0)),
            pl.BlockSpec(cand_ids.shape, lambda i: (0, 0)),
        ],
        out_specs=[pl.BlockSpec((tb, hk), row)] * 3,
        out_shape=[jax.ShapeDtypeStruct((t, hk), F32)] * 3,
        compiler_params=pltpu.CompilerParams(dimension_semantics=("arbitrary",),
                                             vmem_limit_bytes=VMEM_LIMIT_BYTES),
        name="peer_route",
    )(x1, wq_bf, keys_bf, cand_ids)


def _fp8_scale(bound):
    return jnp.exp2(jnp.floor(jnp.log2(FP8_TARGET / jnp.maximum(bound, FP8_TINY))))


def _peer_kernel(x_ref, i1_ref, i2_ref, gate_ref, ut_ref, stat_ref, v_ref, g_ref, b_ref, o_ref,
                 gs_ref, x8_ref, hscale_ref, wscale_ref, acc_ref, *, tb, ec, pitch, unroll):
    e = pl.program_id(1)
    ncl = ec // PEER_NKEYS

    @pl.when(e == 0)
    def _():
        x = x_ref[...]
        sx = _fp8_scale(jnp.max(jnp.max(jnp.abs(x), axis=0, keepdims=True), axis=1, keepdims=True))
        x8_ref[...] = (x * sx).astype(FP8)
        hscale_ref[...] = jnp.broadcast_to(stat_ref[0:1, 0:1] / sx, hscale_ref.shape)
        row_norm = jnp.sqrt(jnp.sum(x * x, axis=1, keepdims=True))
        sw = _fp8_scale(PEER_HEADS * row_norm * stat_ref[0:1, 1:2])
        wscale_ref[...] = jnp.broadcast_to(sw, wscale_ref.shape)
        acc_ref[...] = jnp.zeros_like(acc_ref)
        key_id = lax.broadcasted_iota(jnp.int32, (PEER_NKEYS, PEER_NKEYS), 0).astype(F32)

        def build(blk, carry):
            for r in range(unroll):
                t = blk * unroll + r
                i1 = jnp.broadcast_to(i1_ref[pl.ds(t, 1), :], (PEER_NKEYS, PEER_NKEYS))
                i2 = jnp.broadcast_to(i2_ref[pl.ds(t, 1), :], (PEER_NKEYS, PEER_NKEYS))
                gt = jnp.broadcast_to(0.5 * gate_ref[pl.ds(t, 1), :] * wscale_ref[pl.ds(t, 1), :]
                                      * hscale_ref[0:1, :], (PEER_NKEYS, PEER_NKEYS))
                a_t = jnp.where(i1 == key_id, gt, 0.0).astype(BF16)
                b_t = jnp.where(i2 == key_id, 1.0, 0.0).astype(BF16)
                g_tok = lax.dot_general(a_t, b_t, (((1,), (1,)), ((), ())), preferred_element_type=F32)
                g_rnd = g_tok.astype(BF16).astype(F32).reshape(PEER_NKEYS // PEER_PAIR, PEER_PAIR, PEER_NKEYS)
                hi = lax.bitcast_convert_type(g_rnd[:, :SUBLANES, :].reshape(PEER_PLANES, PEER_NKEYS), jnp.uint32)
                lo = lax.bitcast_convert_type(g_rnd[:, SUBLANES:, :].reshape(PEER_PLANES, PEER_NKEYS),
                                              jnp.uint32) >> 16
                gs_ref[pl.ds(t, PEER_PLANES, stride=pitch), :] = hi | lo
            return carry

        lax.fori_loop(0, tb // unroll, build, 0)

    d = jnp.dot(x8_ref[...], ut_ref[...], preferred_element_type=F32)
    act = d * (1.0 + lax.erf(d * (hscale_ref[0:1, 0:1] * 2.0 ** -0.5)))
    parts = [None] * ncl
    for grp in range(ncl // PEER_PAIR):
        for i in range(SUBLANES):
            plane = (e * (ncl // PEER_PAIR) + grp) * SUBLANES + i
            word = gs_ref[pl.ds(pl.multiple_of(plane * pitch, SUBLANES), tb), :]
            halves = (lax.bitcast_convert_type(word & jnp.uint32(0xFFFF0000), F32),
                      lax.bitcast_convert_type(word << 16, F32))
            for half, gval in enumerate(halves):
                cl = grp * PEER_PAIR + half * SUBLANES + i
                parts[cl] = act[:, cl * PEER_NKEYS:(cl + 1) * PEER_NKEYS] * gval
    w = jnp.concatenate(parts, axis=1).astype(FP8)
    acc_ref[...] += jnp.dot(w, v_ref[...], preferred_element_type=F32)

    @pl.when(e == pl.num_programs(1) - 1)
    def _():
        peer = acc_ref[...] * (stat_ref[0:1, 2:3] / wscale_ref[:, 0:1])
        o_ref[...] = _layer_norm_rows(DEEPNORM_ALPHA * x_ref[...] + peer, g_ref[...], b_ref[...])


def _peer(x1, i1, i2, gate, ut_f8, stats, v_f8, ln_g, ln_b, *, tb=512, ec=2048, unroll=128):
    t = x1.shape[0]
    n_exp = v_f8.shape[0]
    hk = PEER_HEADS * PEER_TOPK
    pitch = tb + SUBLANES
    row = lambda i, e: (i, 0)
    const = lambda i, e: (0, 0)
    return pl.pallas_call(
        functools.partial(_peer_kernel, tb=tb, ec=ec, pitch=pitch, unroll=unroll),
        grid=(t // tb, n_exp // ec),
        in_specs=[
            pl.BlockSpec((tb, D_MODEL), row),
            pl.BlockSpec((tb, hk), row),
            pl.BlockSpec((tb, hk), row),
            pl.BlockSpec((tb, hk), row),
            pl.BlockSpec((D_MODEL, ec), lambda i, e: (0, e)),
            pl.BlockSpec((1, LANES), const),
            pl.BlockSpec((ec, D_MODEL), lambda i, e: (e, 0)),
            pl.BlockSpec((1, D_MODEL), const),
            pl.BlockSpec((1, D_MODEL), const),
        ],
        out_specs=pl.BlockSpec((tb, D_MODEL), row),
        out_shape=jax.ShapeDtypeStruct((t, D_MODEL), F32),
        scratch_shapes=[
            pltpu.VMEM((PEER_PLANES * pitch, PEER_NKEYS), jnp.uint32),
            pltpu.VMEM((tb, D_MODEL), FP8),
            pltpu.VMEM((SUBLANES, LANES), F32),
            pltpu.VMEM((tb, LANES), F32),
            pltpu.VMEM((tb, D_MODEL), F32),
        ],
        compiler_params=pltpu.CompilerParams(dimension_semantics=("arbitrary", "arbitrary"),
                                             vmem_limit_bytes=VMEM_LIMIT_BYTES),
        name="peer_dense",
    )(x1, i1, i2, gate, ut_f8, stats, v_f8, ln_g, ln_b)


def _rope_tables(seq):
    rows = seq // GRID_W
    row = jnp.repeat(jnp.arange(rows, dtype=F32), GRID_W)
    col = jnp.tile(jnp.arange(GRID_W, dtype=F32), rows)
    n_freq = ATT_DH // 4
    inv_freq = ROPE_THETA ** (-jnp.arange(n_freq, dtype=F32) / n_freq)
    ang = jnp.concatenate([row[:, None] * inv_freq, col[:, None] * inv_freq], axis=-1)
    cos, sin = jnp.cos(ang), jnp.sin(ang)
    return jnp.concatenate([cos, cos], axis=-1), jnp.concatenate([-sin, sin], axis=-1)


def _layer(x2d, batch, seq, cos2, sin2, p):
    rq, rk, rv, rg, aq, ak, av, y_sgu, gates = _inproj(
        x2d, seq, p["w_in"], cos2, sin2, p["b_gate"], p["qn"], p["kn"], p["sgu_ln_g"], p["sgu_ln_b"],
        p["sgu_w"], p["sgu_bt"])
    y_ret = _retention(rq, rk, rv, rg, p["lg"], batch, seq)
    y_att = _attention(aq, ak, jnp.transpose(av), p["att_score_bound"], batch, seq)
    x1 = _merge(x2d, y_ret, y_att, y_sgu, gates, p["w_branch"], p["w_out"], p["ln1_g"], p["ln1_b"])
    i1, i2, gate = _route(x1, p["peer_wq"], p["peer_keys"])
    return _peer(x1, i1, i2, gate, p["peer_ut"], p["peer_stats"], p["peer_v"], p["ln2_g"], p["ln2_b"])


def _prep_layer(l, w_in, b_gate, ret_decay_fwd, ret_decay_bwd, attn_q_norm, attn_k_norm,
                sgu_ln_g, sgu_ln_b, sgu_w, sgu_b, w_branch, w_out, ln1_g, ln1_b,
                peer_wq, peer_keys, peer_u, peer_v, ln2_g, ln2_b):
    half = PEER_DQ // 2
    u_norm = jnp.sqrt(jnp.max(jnp.sum(jnp.square(peer_u[l]), axis=1)))
    u_scale = _fp8_scale(u_norm)
    v_scale = _fp8_scale(jnp.max(jnp.abs(peer_v[l])))
    stats = jnp.zeros((1, LANES), F32).at[0, 0].set(1.0 / u_scale).at[0, 1].set(u_norm).at[0, 2].set(1.0 / v_scale)
    return dict(
        w_in=w_in[l].astype(BF16),
        b_gate=b_gate[l].reshape(1, GATE_W),
        lg=jnp.stack([jax.nn.log_sigmoid(ret_decay_fwd[l].astype(F32)),
                      jax.nn.log_sigmoid(ret_decay_bwd[l].astype(F32))]),
        att_score_bound=(ATT_DH * ATT_Q_SCALE * ATT_BOUND_SLACK
                         * jnp.max(jnp.abs(attn_q_norm[l])) * jnp.max(jnp.abs(attn_k_norm[l]))),
        qn=attn_q_norm[l].reshape(1, ATT_DH),
        kn=attn_k_norm[l].reshape(1, ATT_DH),
        sgu_ln_g=sgu_ln_g[l].reshape(1, SGU_WIDTH),
        sgu_ln_b=sgu_ln_b[l].reshape(1, SGU_WIDTH),
        sgu_w=sgu_w[l].astype(BF16),
        sgu_bt=jnp.transpose(sgu_b[l]),
        w_branch=w_branch[l].astype(BF16),
        w_out=w_out[l].astype(BF16),
        ln1_g=ln1_g[l].reshape(1, D_MODEL),
        ln1_b=ln1_b[l].reshape(1, D_MODEL),
        peer_wq=peer_wq[l].astype(BF16),
        peer_keys=peer_keys[l].reshape(PEER_HEADS * 2, PEER_NKEYS, half).astype(BF16),
        peer_ut=jnp.transpose(peer_u[l] * u_scale).astype(FP8),
        peer_stats=stats,
        peer_v=(peer_v[l] * v_scale).astype(FP8),
        ln2_g=ln2_g[l].reshape(1, D_MODEL),
        ln2_b=ln2_b[l].reshape(1, D_MODEL),
    )


def kernel(x_prompt, x_sample, w_in, b_gate, ret_decay_fwd, ret_decay_bwd, attn_q_norm, attn_k_norm,
           sgu_ln_g, sgu_ln_b, sgu_w, sgu_b, w_branch, w_out, ln1_g, ln1_b,
           peer_wq, peer_keys, peer_u, peer_v, ln2_g, ln2_b):
    weights = (w_in, b_gate, ret_decay_fwd, ret_decay_bwd, attn_q_norm, attn_k_norm,
               sgu_ln_g, sgu_ln_b, sgu_w, sgu_b, w_branch, w_out, ln1_g, ln1_b,
               peer_wq, peer_keys, peer_u, peer_v, ln2_g, ln2_b)
    layers = [_prep_layer(l, *weights) for l in range(DEPTH)]
    outs = []
    for x in (x_prompt, x_sample):
        batch, seq, _ = x.shape
        cos2, sin2 = _rope_tables(seq)
        h = x.reshape(batch * seq, D_MODEL)
        for p in layers:
            h = _layer(h, batch, seq, cos2, sin2, p)
        outs.append(h.reshape(batch, seq, D_MODEL))
    return tuple(outs)
```

```python
import functools
import math

import jax
import jax.numpy as jnp
import numpy as np
from jax import lax
from jax.experimental import pallas as pl
from jax.experimental.pallas import tpu as pltpu

F32 = jnp.float32
BF16 = jnp.bfloat16
FP8 = jnp.float8_e4m3fn

D_MODEL = 1024
DEPTH = 2
GRID_W = 64
CHUNK = 128
RET_HEADS = 4
RET_DK = 128
RET_DV = 256
ATT_HEADS = 8
ATT_KV_HEADS = 2
ATT_DH = 128
ATT_GROUP = ATT_HEADS // ATT_KV_HEADS
SGU_GROUPS = 4
SGU_WIDTH = 1024
SGU_GW = SGU_WIDTH // SGU_GROUPS
PEER_HEADS = 8
PEER_NKEYS = 128
PEER_DQ = 256
PEER_TOPK = 16
PEER_PLANES = PEER_NKEYS // 2
N_BRANCH = 3
ROPE_THETA = 10000.0
DEEPNORM_ALPHA = (2 * DEPTH) ** 0.25
LN_EPS = 1e-5
RMS_EPS = 1e-6

RET_QK_W = RET_HEADS * RET_DK
RET_V_W = RET_HEADS * RET_DV
ATT_Q_W = ATT_HEADS * ATT_DH
ATT_KV_W = ATT_KV_HEADS * ATT_DH
GATE_W = N_BRANCH * D_MODEL
SPLITS = (RET_QK_W, RET_QK_W, RET_V_W, RET_V_W, ATT_Q_W, ATT_KV_W, ATT_KV_W, SGU_WIDTH, SGU_WIDTH, GATE_W)
OFFS = tuple(int(sum(SPLITS[:i])) for i in range(len(SPLITS) + 1))
IN_W = OFFS[-1]

LANES = 128
SUBLANES = 8
PEER_PAIR = 2 * SUBLANES
VMEM_LIMIT_BYTES = 56 * 1024 * 1024

NEG_INF = float("-inf")
FP8_TARGET = 224.0
FP8_TINY = 1e-30
ATT_Q_SCALE = ATT_DH ** -0.5 * math.log2(math.e)
ATT_LOGIT_LIMIT = 64.0
ATT_BOUND_SLACK = 1.02


def _gelu(x):
    return 0.5 * x * (1.0 + lax.erf(x * (2.0 ** -0.5)))


def _rope(x, cos2, sin2):
    return x * cos2 + pltpu.roll(x, ATT_DH // 2, 1) * sin2


def _layer_norm_rows(x, g, b):
    mu = jnp.mean(x, axis=-1, keepdims=True)
    xc = x - mu
    var = jnp.mean(xc * xc, axis=-1, keepdims=True)
    return xc * lax.rsqrt(var + LN_EPS) * g + b


def _inproj_kernel(x_ref, w_ref, cos_ref, sin_ref, bgate_ref, qn_ref, kn_ref, lng_ref, lnb_ref,
                   sw_ref, sb_ref,
                   rq_ref, rk_ref, rv_ref, rg_ref, aq_ref, ak_ref, av_ref, ysgu_ref, gates_ref):
    tb = x_ref.shape[0]
    xb = x_ref[...].astype(BF16)
    cos2 = cos_ref[...]
    sin2 = sin_ref[...]

    def proj(i):
        return jnp.dot(xb, w_ref[:, OFFS[i]:OFFS[i + 1]], preferred_element_type=F32)

    def rope_heads(h, o_ref, nheads, width, scale):
        for hd in range(nheads):
            sl = slice(hd * width, (hd + 1) * width)
            o_ref[:, sl] = (_rope(h[:, sl], cos2, sin2) * scale).astype(BF16)

    def norm_rope_heads(h, gain, o_ref, nheads, scale):
        for hd in range(nheads):
            sl = slice(hd * ATT_DH, (hd + 1) * ATT_DH)
            t = h[:, sl]
            t = t * lax.rsqrt(jnp.mean(t * t, axis=-1, keepdims=True) + RMS_EPS) * gain
            o_ref[:, sl] = (_rope(t, cos2, sin2) * scale).astype(BF16)

    carry = {}

    def sgu_u(h):
        carry["u"] = _gelu(h)

    def sgu_v(h):
        u = carry["u"]
        vn = _layer_norm_rows(_gelu(h), lng_ref[...], lnb_ref[...]).astype(BF16)
        sb = sb_ref[...]
        for c in range(tb // CHUNK):
            rows = slice(c * CHUNK, (c + 1) * CHUNK)
            for g in range(SGU_GROUPS):
                cols = slice(g * SGU_GW, (g + 1) * SGU_GW)
                mixed = jnp.dot(sw_ref[g], vn[rows, cols], preferred_element_type=F32) + sb[:, g:g + 1]
                ysgu_ref[rows, cols] = (u[rows, cols] * mixed).astype(BF16)

    def store(o_ref, fn):
        def epilogue(h):
            o_ref[...] = fn(h).astype(BF16)
        return epilogue

    sections = [
        (7, sgu_u),
        (8, sgu_v),
        (4, lambda h: norm_rope_heads(h, qn_ref[...], aq_ref, ATT_HEADS, ATT_Q_SCALE)),
        (9, store(gates_ref, lambda h: jax.nn.sigmoid(h + bgate_ref[...]))),
        (0, lambda h: rope_heads(h, rq_ref, RET_HEADS, RET_DK, 1.0)),
        (5, lambda h: norm_rope_heads(h, kn_ref[...], ak_ref, ATT_KV_HEADS, 1.0)),
        (6, store(av_ref, lambda h: h)),
        (1, lambda h: rope_heads(h, rk_ref, RET_HEADS, RET_DK, RET_DK ** -0.5)),
        (2, store(rv_ref, lambda h: h)),
        (3, store(rg_ref, lambda h: h * jax.nn.sigmoid(h))),
    ]
    h_next = proj(sections[0][0])
    for pos, (_, epilogue) in enumerate(sections):
        h = h_next
        if pos + 1 < len(sections):
            h_next = proj(sections[pos + 1][0])
        epilogue(h)


def _inproj(x2d, seq, w_bf, cos2, sin2, b_gate, qn, kn, lng, lnb, sgu_w_bf, sgu_bt, *, tb=256):
    t = x2d.shape[0]
    nblk_seq = seq // tb
    const = lambda i: (0, 0)
    row = lambda i: (i, 0)
    pos = lambda i: (i % nblk_seq, 0)
    outs = [(RET_QK_W, BF16), (RET_QK_W, BF16), (RET_V_W, BF16), (RET_V_W, BF16), (ATT_Q_W, BF16),
            (ATT_KV_W, BF16), (ATT_KV_W, BF16), (SGU_WIDTH, BF16), (GATE_W, BF16)]
    return pl.pallas_call(
        _inproj_kernel,
        grid=(t // tb,),
        in_specs=[
            pl.BlockSpec((tb, D_MODEL), row),
            pl.BlockSpec((D_MODEL, IN_W), const, pipeline_mode=pl.Buffered(1)),
            pl.BlockSpec((tb, ATT_DH), pos),
            pl.BlockSpec((tb, ATT_DH), pos),
            pl.BlockSpec((1, GATE_W), const),
            pl.BlockSpec((1, ATT_DH), const),
            pl.BlockSpec((1, ATT_DH), const),
            pl.BlockSpec((1, SGU_WIDTH), const),
            pl.BlockSpec((1, SGU_WIDTH), const),
            pl.BlockSpec((SGU_GROUPS, CHUNK, CHUNK), lambda i: (0, 0, 0)),
            pl.BlockSpec((CHUNK, SGU_GROUPS), const),
        ],
        out_specs=[pl.BlockSpec((tb, w), row) for w, _ in outs],
        out_shape=[jax.ShapeDtypeStruct((t, w), d) for w, d in outs],
        compiler_params=pltpu.CompilerParams(dimension_semantics=("arbitrary",),
                                             vmem_limit_bytes=VMEM_LIMIT_BYTES),
        name="inproj",
    )(x2d, w_bf, cos2, sin2, b_gate, qn, kn, lng, lnb, sgu_w_bf, sgu_bt)


def _retention_kernel(lg_ref, q_ref, k_ref, v_ref, *rest, rb, forward):
    if forward:
        g_ref, yb_ref, o_ref, state_ref = rest
    else:
        o_ref, state_ref = rest
    nchunk = rb // CHUNK

    @pl.when(pl.program_id(1) == 0)
    def _():
        state_ref[...] = jnp.zeros_like(state_ref)

    n_i = lax.broadcasted_iota(jnp.int32, (CHUNK, CHUNK), 0)
    m_i = lax.broadcasted_iota(jnp.int32, (CHUNK, CHUNK), 1)
    col = lax.broadcasted_iota(jnp.int32, (CHUNK, 1), 0).astype(F32)
    dist = ((n_i - m_i) if forward else (m_i - n_i)).astype(F32)
    live = (n_i >= m_i) if forward else (m_i > n_i)

    for hd in range(RET_HEADS):
        lg = lg_ref[0 if forward else 1, hd]
        decay = jnp.where(live, jnp.exp(lg * jnp.maximum(dist, 0.0)), 0.0)
        zeta = jnp.exp(lg * ((CHUNK - 1.0 - col) if forward else col))
        xi = jnp.exp(lg * ((col + 1.0) if forward else (CHUNK - col)))
        chunk_decay = jnp.exp(lg * CHUNK)
        qk_cols = slice(hd * RET_DK, (hd + 1) * RET_DK)
        v_cols = slice(hd * RET_DV, (hd + 1) * RET_DV)
        order = list(range(nchunk) if forward else reversed(range(nchunk)))
        qs, vs, scs, upds = {}, {}, {}, {}
        for c in order:
            rows = slice(c * CHUNK, (c + 1) * CHUNK)
            q = qs[c] = q_ref[rows, qk_cols]
            k = k_ref[rows, qk_cols]
            v = vs[c] = v_ref[rows, v_cols]
            sc = lax.dot_general(q, k, (((1,), (1,)), ((), ())), preferred_element_type=F32) * decay
            scs[c] = sc.astype(BF16)
            vz = (v.astype(F32) * zeta).astype(BF16)
            upds[c] = lax.dot_general(k, vz, (((0,), (0,)), ((), ())), preferred_element_type=F32)
        for c in order:
            rows = slice(c * CHUNK, (c + 1) * CHUNK)
            q, v = qs[c], vs[c]
            st = state_ref[hd]
            lhs = jnp.concatenate([scs[c], (q.astype(F32) * xi).astype(BF16)], axis=1)
            y = jnp.dot(lhs, jnp.concatenate([v, st.astype(BF16)], axis=0), preferred_element_type=F32)
            state_ref[hd] = st * chunk_decay + upds[c]
            if forward:
                y = y + yb_ref[rows, v_cols]
                mu = jnp.mean(y, axis=-1, keepdims=True)
                yc = y - mu
                var = jnp.mean(yc * yc, axis=-1, keepdims=True)
                yn = yc * lax.rsqrt(var + LN_EPS)
                o_ref[rows, v_cols] = (g_ref[rows, v_cols].astype(F32) * yn).astype(BF16)
            else:
                o_ref[rows, v_cols] = y


def _retention_pass(lg, rq, rk, rv, extra, batch, seq, *, rb, forward):
    t = rq.shape[0]
    ns = seq // rb
    if forward:
        blk = lambda b, s, lg_ref: (b * ns + s, 0)
    else:
        blk = lambda b, s, lg_ref: (b * ns + ns - 1 - s, 0)
    grid_spec = pltpu.PrefetchScalarGridSpec(
        num_scalar_prefetch=1,
        grid=(batch, ns),
        in_specs=[pl.BlockSpec((rb, RET_QK_W), blk), pl.BlockSpec((rb, RET_QK_W), blk),
                  pl.BlockSpec((rb, RET_V_W), blk)] + [pl.BlockSpec((rb, RET_V_W), blk)] * len(extra),
        out_specs=pl.BlockSpec((rb, RET_V_W), blk),
        scratch_shapes=[pltpu.VMEM((RET_HEADS, RET_DK, RET_DV), F32)],
    )
    return pl.pallas_call(
        functools.partial(_retention_kernel, rb=rb, forward=forward),
        grid_spec=grid_spec,
        out_shape=jax.ShapeDtypeStruct((t, RET_V_W), BF16 if forward else F32),
        compiler_params=pltpu.CompilerParams(dimension_semantics=("arbitrary", "arbitrary"),
                                             vmem_limit_bytes=VMEM_LIMIT_BYTES),
        name="retention_fwd" if forward else "retention_bwd",
    )(lg, rq, rk, rv, *extra)


def _retention(rq, rk, rv, rg, lg, batch, seq, *, rb=1024):
    y_bwd = _retention_pass(lg, rq, rk, rv, (), batch, seq, rb=rb, forward=False)
    return _retention_pass(lg, rq, rk, rv, (rg, y_bwd), batch, seq, rb=rb, forward=True)


def _attention_kernel(q_ref, k_ref, vt_ref, o_ref, m_ref, l_ref, acc_ref, *, bounded):
    ki = pl.program_id(3)

    @pl.when(ki == 0)
    def _():
        if not bounded:
            m_ref[...] = jnp.full_like(m_ref, NEG_INF)
        l_ref[...] = jnp.zeros_like(l_ref)
        acc_ref[...] = jnp.zeros_like(acc_ref)

    k = k_ref[...]
    vt = vt_ref[...]
    def scores(g):
        q = q_ref[:, g * ATT_DH:(g + 1) * ATT_DH]
        return lax.dot_general(k, q, (((1,), (1,)), ((), ())), preferred_element_type=F32)

    st_next = scores(0)
    for g in range(ATT_GROUP):
        st = st_next
        if g + 1 < ATT_GROUP:
            st_next = scores(g + 1)
        if bounded:
            pt = jnp.exp2(st)
            l_ref[g:g + 1, :] += jnp.sum(pt, axis=0, keepdims=True)
            acc_ref[g] += jnp.dot(vt, pt.astype(BF16), preferred_element_type=F32)
            continue
        m_prev = m_ref[g:g + 1, :]
        m_new = jnp.maximum(m_prev, jnp.max(st, axis=0, keepdims=True))
        alpha = jnp.exp2(m_prev - m_new)
        pt = jnp.exp2(st - m_new)
        l_ref[g:g + 1, :] = alpha * l_ref[g:g + 1, :] + jnp.sum(pt, axis=0, keepdims=True)
        acc_ref[g] = alpha * acc_ref[g] + jnp.dot(vt, pt.astype(BF16), preferred_element_type=F32)
        m_ref[g:g + 1, :] = m_new

    @pl.when(ki == pl.num_programs(3) - 1)
    def _():
        for g in range(ATT_GROUP):
            out = acc_ref[g] / l_ref[g:g + 1, :]
            o_ref[:, g * ATT_DH:(g + 1) * ATT_DH] = out.T.astype(BF16)


def _attention(aq, ak, avt, score_bound, batch, seq):
    return lax.cond(score_bound <= ATT_LOGIT_LIMIT,
                    functools.partial(_attention_call, batch=batch, seq=seq, bounded=True),
                    functools.partial(_attention_call, batch=batch, seq=seq, bounded=False),
                    aq, ak, avt)


def _attention_call(aq, ak, avt, *, batch, seq, bounded, tq=512, tk=4096):
    t = aq.shape[0]
    nq = seq // tq
    nk = seq // tk
    gw = ATT_GROUP * ATT_DH
    return pl.pallas_call(
        functools.partial(_attention_kernel, bounded=bounded),
        grid=(batch, ATT_KV_HEADS, nq, nk),
        in_specs=[
            pl.BlockSpec((tq, gw), lambda b, g, qi, ki: (b * nq + qi, g)),
            pl.BlockSpec((tk, ATT_DH), lambda b, g, qi, ki: (b * nk + ki, g)),
            pl.BlockSpec((ATT_DH, tk), lambda b, g, qi, ki: (g, b * nk + ki)),
        ],
        out_specs=pl.BlockSpec((tq, gw), lambda b, g, qi, ki: (b * nq + qi, g)),
        out_shape=jax.ShapeDtypeStruct((t, ATT_Q_W), BF16),
        scratch_shapes=[
            pltpu.VMEM((SUBLANES, tq), F32),
            pltpu.VMEM((SUBLANES, tq), F32),
            pltpu.VMEM((ATT_GROUP, ATT_DH, tq), F32),
        ],
        compiler_params=pltpu.CompilerParams(
            dimension_semantics=("arbitrary", "arbitrary", "arbitrary", "arbitrary"),
            vmem_limit_bytes=VMEM_LIMIT_BYTES),
        name="attention_bounded" if bounded else "attention",
    )(aq, ak, avt)


def _merge_kernel(x_ref, yr_ref, ya_ref, ys_ref, gates_ref, wb_ref, wo_ref, g_ref, b_ref, o_ref, *, nsplit):
    sub = x_ref.shape[0] // nsplit

    def branches(r):
        rows = slice(r * sub, (r + 1) * sub)
        merged = None
        for i, y_ref in enumerate((yr_ref, ya_ref, ys_ref)):
            gate = gates_ref[rows, i * D_MODEL:(i + 1) * D_MODEL].astype(F32)
            term = gate * jnp.dot(y_ref[rows, :], wb_ref[i], preferred_element_type=F32)
            merged = term if merged is None else merged + term
        return merged.astype(BF16)

    nxt = branches(0)
    for r in range(nsplit):
        rows = slice(r * sub, (r + 1) * sub)
        cur = nxt
        if r + 1 < nsplit:
            nxt = branches(r + 1)
        proj = jnp.dot(cur, wo_ref[...], preferred_element_type=F32)
        o_ref[rows, :] = _layer_norm_rows(DEEPNORM_ALPHA * x_ref[rows, :] + proj, g_ref[...], b_ref[...])


def _merge(x2d, y_ret, y_att, y_sgu, gates, wb_bf, wo_bf, ln_g, ln_b, *, tb=1024, nsplit=4):
    t = x2d.shape[0]
    row = lambda i: (i, 0)
    const = lambda i: (0, 0)
    return pl.pallas_call(
        functools.partial(_merge_kernel, nsplit=nsplit),
        grid=(t // tb,),
        in_specs=[
            pl.BlockSpec((tb, D_MODEL), row),
            pl.BlockSpec((tb, D_MODEL), row),
            pl.BlockSpec((tb, D_MODEL), row),
            pl.BlockSpec((tb, D_MODEL), row),
            pl.BlockSpec((tb, GATE_W), row),
            pl.BlockSpec((N_BRANCH, D_MODEL, D_MODEL), lambda i: (0, 0, 0)),
            pl.BlockSpec((D_MODEL, D_MODEL), const),
            pl.BlockSpec((1, D_MODEL), const),
            pl.BlockSpec((1, D_MODEL), const),
        ],
        out_specs=pl.BlockSpec((tb, D_MODEL), row),
        out_shape=jax.ShapeDtypeStruct((t, D_MODEL), F32),
        compiler_params=pltpu.CompilerParams(dimension_semantics=("arbitrary",),
                                             vmem_limit_bytes=VMEM_LIMIT_BYTES),
        name="merge",
    )(x2d, y_ret, y_att, y_sgu, gates, wb_bf, wo_bf, ln_g, ln_b)


PEER_PAD_ID = 1024.0


def _peer_candidate_ids():
    pairs = [(0, b) for b in range(PEER_TOPK)]
    pairs += [(a, b) for a in range(1, SUBLANES) for b in range(SUBLANES)]
    pairs += [(a, 0) for a in range(SUBLANES, PEER_TOPK)]
    ids = [float(a * PEER_TOPK + b) if (a + 1) * (b + 1) <= PEER_TOPK else PEER_PAD_ID for a, b in pairs]
    return np.asarray(ids, np.float32)


def _topk_rows(v, ids, k, group):
    nblk = v.shape[0] // SUBLANES
    lists = []
    for g0 in range(0, nblk, group):
        lv = [v[b * SUBLANES:(b + 1) * SUBLANES] for b in range(g0, g0 + group)]
        li = [ids[b * SUBLANES:(b + 1) * SUBLANES] for b in range(g0, g0 + group)]
        for rnd in range(group):
            for a in range(rnd % 2, group - 1, 2):
                swap = lv[a + 1] > lv[a]
                lv[a], lv[a + 1] = jnp.where(swap, lv[a + 1], lv[a]), jnp.where(swap, lv[a], lv[a + 1])
                li[a], li[a + 1] = jnp.where(swap, li[a + 1], li[a]), jnp.where(swap, li[a], li[a + 1])
        lists.append((lv, li))
    vals, idxs = [], []
    for _ in range(k):
        head = functools.reduce(jnp.maximum, [lv[0] for lv, _ in lists])
        m = jnp.max(head, axis=0, keepdims=True)
        cand = functools.reduce(jnp.minimum,
                                [jnp.where(lv[0] == m, li[0], 2.0 * PEER_PAD_ID) for lv, li in lists])
        idx = jnp.min(cand, axis=0, keepdims=True)
        vals.append(m)
        idxs.append(idx)
        for lv, li in lists:
            won = li[0] == idx
            for j in range(group - 1):
                lv[j] = jnp.where(won, lv[j + 1], lv[j])
                li[j] = jnp.where(won, li[j + 1], li[j])
            lv[group - 1] = jnp.where(won, NEG_INF, lv[group - 1])
    return vals, idxs


def _route_kernel(x_ref, wq_ref, keys_ref, cid_ref, i1_ref, i2_ref, gate_ref):
    tb = x_ref.shape[0]
    q = jnp.dot(x_ref[...].astype(BF16), wq_ref[...], preferred_element_type=F32).astype(BF16)
    half = PEER_DQ // 2
    key_row = lax.broadcasted_iota(jnp.int32, (PEER_NKEYS, tb), 0).astype(F32)
    arow = lax.broadcasted_iota(jnp.int32, (PEER_TOPK, tb), 0).astype(F32)
    cand_ids = cid_ref[...]
    i1_rows, i2_rows, gate_rows = [], [], []
    for h in range(PEER_HEADS):
        sub = []
        for p in range(2):
            qs = q[:, (h * 2 + p) * half:(h * 2 + p + 1) * half]
            sc = lax.dot_general(keys_ref[h * 2 + p], qs, (((1,), (1,)), ((), ())),
                                 preferred_element_type=F32)
            sub.append(_topk_rows(sc, key_row, PEER_TOPK, group=8))
        (s1, i1), (s2, i2) = sub
        s1m = jnp.concatenate(s1, axis=0)
        s2m = jnp.concatenate(s2, axis=0)
        i1m = jnp.concatenate(i1, axis=0)
        i2m = jnp.concatenate(i2, axis=0)
        pieces = [s1[0] + s2m[:SUBLANES], s1[0] + s2m[SUBLANES:]]
        pieces += [s1[a] + s2m[:SUBLANES] for a in range(1, SUBLANES)]
        pieces += [s1m[SUBLANES:] + s2[0]]
        cand = jnp.where(cand_ids < PEER_PAD_ID, jnp.concatenate(pieces, axis=0), NEG_INF)
        top_s, top_pos = _topk_rows(cand, cand_ids, PEER_TOPK, group=5)
        top_s = jnp.concatenate(top_s, axis=0)
        e = jnp.exp(top_s - top_s[0:1, :])
        gate_rows.append(e / jnp.sum(e, axis=0, keepdims=True))
        for pos in top_pos:
            a = jnp.floor(pos * (1.0 / PEER_TOPK))
            b = pos - a * PEER_TOPK
            i1_rows.append(jnp.sum(jnp.where(arow == a, i1m, 0.0), axis=0, keepdims=True))
            i2_rows.append(jnp.sum(jnp.where(arow == b, i2m, 0.0), axis=0, keepdims=True))
    i1_ref[...] = jnp.concatenate(i1_rows, axis=0).T
    i2_ref[...] = jnp.concatenate(i2_rows, axis=0).T
    gate_ref[...] = jnp.concatenate(gate_rows, axis=0).T


def _route(x1, wq_bf, keys_bf, *, tb=128):
    t = x1.shape[0]
    hk = PEER_HEADS * PEER_TOPK
    row = lambda i: (i, 0)
    ids = _peer_candidate_ids()
    cand_ids = jnp.asarray(np.ascontiguousarray(np.broadcast_to(ids[:, None], (ids.shape[0], tb))))
    return pl.pallas_call(
        _route_kernel,
        grid=(t // tb,),
        in_specs=[
            pl.BlockSpec((tb, D_MODEL), row),
            pl.BlockSpec((D_MODEL, PEER_HEADS * PEER_DQ), lambda i: (0, 0)),
            pl.BlockSpec((PEER_HEADS * 2, PEER_NKEYS, PEER_DQ // 2), lambda i: (0, 0, 0)),
            pl.BlockSpec(cand_ids.shape, lambda i: (0, 0)),
        ],
        out_specs=[pl.BlockSpec((tb, hk), row)] * 3,
        out_shape=[jax.ShapeDtypeStruct((t, hk), F32)] * 3,
        compiler_params=pltpu.CompilerParams(dimension_semantics=("arbitrary",),
                                             vmem_limit_bytes=VMEM_LIMIT_BYTES),
        name="peer_route",
    )(x1, wq_bf, keys_bf, cand_ids)


def _fp8_scale(bound):
    return jnp.exp2(jnp.floor(jnp.log2(FP8_TARGET / jnp.maximum(bound, FP8_TINY))))


def _peer_kernel(x_ref, i1_ref, i2_ref, gate_ref, ut_ref, stat_ref, v_ref, g_ref, b_ref, o_ref,
                 gs_ref, x8_ref, hscale_ref, wscale_ref, acc_ref, *, tb, ec, pitch, unroll):
    e = pl.program_id(1)
    ncl = ec // PEER_NKEYS

    @pl.when(e == 0)
    def _():
        x = x_ref[...]
        sx = _fp8_scale(jnp.max(jnp.max(jnp.abs(x), axis=0, keepdims=True), axis=1, keepdims=True))
        x8_ref[...] = (x * sx).astype(FP8)
        hscale_ref[...] = jnp.broadcast_to(stat_ref[0:1, 0:1] / sx, hscale_ref.shape)
        row_norm = jnp.sqrt(jnp.sum(x * x, axis=1, keepdims=True))
        sw = _fp8_scale(PEER_HEADS * row_norm * stat_ref[0:1, 1:2])
        wscale_ref[...] = jnp.broadcast_to(sw, wscale_ref.shape)
        acc_ref[...] = jnp.zeros_like(acc_ref)
        key_id = lax.broadcasted_iota(jnp.int32, (PEER_NKEYS, PEER_NKEYS), 0).astype(F32)

        def build(blk, carry):
            for r in range(unroll):
                t = blk * unroll + r
                i1 = jnp.broadcast_to(i1_ref[pl.ds(t, 1), :], (PEER_NKEYS, PEER_NKEYS))
                i2 = jnp.broadcast_to(i2_ref[pl.ds(t, 1), :], (PEER_NKEYS, PEER_NKEYS))
                gt = jnp.broadcast_to(0.5 * gate_ref[pl.ds(t, 1), :] * wscale_ref[pl.ds(t, 1), :]
                                      * hscale_ref[0:1, :], (PEER_NKEYS, PEER_NKEYS))
                a_t = jnp.where(i1 == key_id, gt, 0.0).astype(BF16)
                b_t = jnp.where(i2 == key_id, 1.0, 0.0).astype(BF16)
                g_tok = lax.dot_general(a_t, b_t, (((1,), (1,)), ((), ())), preferred_element_type=F32)
                g_rnd = g_tok.astype(BF16).astype(F32).reshape(PEER_NKEYS // PEER_PAIR, PEER_PAIR, PEER_NKEYS)
                hi = lax.bitcast_convert_type(g_rnd[:, :SUBLANES, :].reshape(PEER_PLANES, PEER_NKEYS), jnp.uint32)
                lo = lax.bitcast_convert_type(g_rnd[:, SUBLANES:, :].reshape(PEER_PLANES, PEER_NKEYS),
                                              jnp.uint32) >> 16
                gs_ref[pl.ds(t, PEER_PLANES, stride=pitch), :] = hi | lo
            return carry

        lax.fori_loop(0, tb // unroll, build, 0)

    d = jnp.dot(x8_ref[...], ut_ref[...], preferred_element_type=F32)
    act = d * (1.0 + lax.erf(d * (hscale_ref[0:1, 0:1] * 2.0 ** -0.5)))
    parts = [None] * ncl
    for grp in range(ncl // PEER_PAIR):
        for i in range(SUBLANES):
            plane = (e * (ncl // PEER_PAIR) + grp) * SUBLANES + i
            word = gs_ref[pl.ds(pl.multiple_of(plane * pitch, SUBLANES), tb), :]
            halves = (lax.bitcast_convert_type(word & jnp.uint32(0xFFFF0000), F32),
                      lax.bitcast_convert_type(word << 16, F32))
            for half, gval in enumerate(halves):
                cl = grp * PEER_PAIR + half * SUBLANES + i
                parts[cl] = act[:, cl * PEER_NKEYS:(cl + 1) * PEER_NKEYS] * gval
    w = jnp.concatenate(parts, axis=1).astype(FP8)
    acc_ref[...] += jnp.dot(w, v_ref[...], preferred_element_type=F32)

    @pl.when(e == pl.num_programs(1) - 1)
    def _():
        peer = acc_ref[...] * (stat_ref[0:1, 2:3] / wscale_ref[:, 0:1])
        o_ref[...] = _layer_norm_rows(DEEPNORM_ALPHA * x_ref[...] + peer, g_ref[...], b_ref[...])


def _peer(x1, i1, i2, gate, ut_f8, stats, v_f8, ln_g, ln_b, *, tb=512, ec=2048, unroll=128):
    t = x1.shape[0]
    n_exp = v_f8.shape[0]
    hk = PEER_HEADS * PEER_TOPK
    pitch = tb + SUBLANES
    row = lambda i, e: (i, 0)
    const = lambda i, e: (0, 0)
    return pl.pallas_call(
        functools.partial(_peer_kernel, tb=tb, ec=ec, pitch=pitch, unroll=unroll),
        grid=(t // tb, n_exp // ec),
        in_specs=[
            pl.BlockSpec((tb, D_MODEL), row),
            pl.BlockSpec((tb, hk), row),
            pl.BlockSpec((tb, hk), row),
            pl.BlockSpec((tb, hk), row),
            pl.BlockSpec((D_MODEL, ec), lambda i, e: (0, e)),
            pl.BlockSpec((1, LANES), const),
            pl.BlockSpec((ec, D_MODEL), lambda i, e: (e, 0)),
            pl.BlockSpec((1, D_MODEL), const),
            pl.BlockSpec((1, D_MODEL), const),
        ],
        out_specs=pl.BlockSpec((tb, D_MODEL), row),
        out_shape=jax.ShapeDtypeStruct((t, D_MODEL), F32),
        scratch_shapes=[
            pltpu.VMEM((PEER_PLANES * pitch, PEER_NKEYS), jnp.uint32),
            pltpu.VMEM((tb, D_MODEL), FP8),
            pltpu.VMEM((SUBLANES, LANES), F32),
            pltpu.VMEM((tb, LANES), F32),
            pltpu.VMEM((tb, D_MODEL), F32),
        ],
        compiler_params=pltpu.CompilerParams(dimension_semantics=("arbitrary", "arbitrary"),
                                             vmem_limit_bytes=VMEM_LIMIT_BYTES),
        name="peer_dense",
    )(x1, i1, i2, gate, ut_f8, stats, v_f8, ln_g, ln_b)


def _rope_tables(seq):
    rows = seq // GRID_W
    row = jnp.repeat(jnp.arange(rows, dtype=F32), GRID_W)
    col = jnp.tile(jnp.arange(GRID_W, dtype=F32), rows)
    n_freq = ATT_DH // 4
    inv_freq = ROPE_THETA ** (-jnp.arange(n_freq, dtype=F32) / n_freq)
    ang = jnp.concatenate([row[:, None] * inv_freq, col[:, None] * inv_freq], axis=-1)
    cos, sin = jnp.cos(ang), jnp.sin(ang)
    return jnp.concatenate([cos, cos], axis=-1), jnp.concatenate([-sin, sin], axis=-1)


def _layer(x2d, batch, seq, cos2, sin2, p):
    rq, rk, rv, rg, aq, ak, av, y_sgu, gates = _inproj(
        x2d, seq, p["w_in"], cos2, sin2, p["b_gate"], p["qn"], p["kn"], p["sgu_ln_g"], p["sgu_ln_b"],
        p["sgu_w"], p["sgu_bt"])
    y_ret = _retention(rq, rk, rv, rg, p["lg"], batch, seq)
    y_att = _attention(aq, ak, jnp.transpose(av), p["att_score_bound"], batch, seq)
    x1 = _merge(x2d, y_ret, y_att, y_sgu, gates, p["w_branch"], p["w_out"], p["ln1_g"], p["ln1_b"])
    i1, i2, gate = _route(x1, p["peer_wq"], p["peer_keys"])
    return _peer(x1, i1, i2, gate, p["peer_ut"], p["peer_stats"], p["peer_v"], p["ln2_g"], p["ln2_b"])


def _prep_layer(l, w_in, b_gate, ret_decay_fwd, ret_decay_bwd, attn_q_norm, attn_k_norm,
                sgu_ln_g, sgu_ln_b, sgu_w, sgu_b, w_branch, w_out, ln1_g, ln1_b,
                peer_wq, peer_keys, peer_u, peer_v, ln2_g, ln2_b):
    half = PEER_DQ // 2
    u_norm = jnp.sqrt(jnp.max(jnp.sum(jnp.square(peer_u[l]), axis=1)))
    u_scale = _fp8_scale(u_norm)
    v_scale = _fp8_scale(jnp.max(jnp.abs(peer_v[l])))
    stats = jnp.zeros((1, LANES), F32).at[0, 0].set(1.0 / u_scale).at[0, 1].set(u_norm).at[0, 2].set(1.0 / v_scale)
    return dict(
        w_in=w_in[l].astype(BF16),
        b_gate=b_gate[l].reshape(1, GATE_W),
        lg=jnp.stack([jax.nn.log_sigmoid(ret_decay_fwd[l].astype(F32)),
                      jax.nn.log_sigmoid(ret_decay_bwd[l].astype(F32))]),
        att_score_bound=(ATT_DH * ATT_Q_SCALE * ATT_BOUND_SLACK
                         * jnp.max(jnp.abs(attn_q_norm[l])) * jnp.max(jnp.abs(attn_k_norm[l]))),
        qn=attn_q_norm[l].reshape(1, ATT_DH),
        kn=attn_k_norm[l].reshape(1, ATT_DH),
        sgu_ln_g=sgu_ln_g[l].reshape(1, SGU_WIDTH),
        sgu_ln_b=sgu_ln_b[l].reshape(1, SGU_WIDTH),
        sgu_w=sgu_w[l].astype(BF16),
        sgu_bt=jnp.transpose(sgu_b[l]),
        w_branch=w_branch[l].astype(BF16),
        w_out=w_out[l].astype(BF16),
        ln1_g=ln1_g[l].reshape(1, D_MODEL),
        ln1_b=ln1_b[l].reshape(1, D_MODEL),
        peer_wq=peer_wq[l].astype(BF16),
        peer_keys=peer_keys[l].reshape(PEER_HEADS * 2, PEER_NKEYS, half).astype(BF16),
        peer_ut=jnp.transpose(peer_u[l] * u_scale).astype(FP8),
        peer_stats=stats,
        peer_v=(peer_v[l] * v_scale).astype(FP8),
        ln2_g=ln2_g[l].reshape(1, D_MODEL),
        ln2_b=ln2_b[l].reshape(1, D_MODEL),
    )


def kernel(x_prompt, x_sample, w_in, b_gate, ret_decay_fwd, ret_decay_bwd, attn_q_norm, attn_k_norm,
           sgu_ln_g, sgu_ln_b, sgu_w, sgu_b, w_branch, w_out, ln1_g, ln1_b,
           peer_wq, peer_keys, peer_u, peer_v, ln2_g, ln2_b):
    weights = (w_in, b_gate, ret_decay_fwd, ret_decay_bwd, attn_q_norm, attn_k_norm,
               sgu_ln_g, sgu_ln_b, sgu_w, sgu_b, w_branch, w_out, ln1_g, ln1_b,
               peer_wq, peer_keys, peer_u, peer_v, ln2_g, ln2_b)
    layers = [_prep_layer(l, *weights) for l in range(DEPTH)]
    outs = []
    for x in (x_prompt, x_sample):
        batch, seq, _ = x.shape
        cos2, sin2 = _rope_tables(seq)
        h = x.reshape(batch * seq, D_MODEL)
        for p in layers:
            h = _layer(h, batch, seq, cos2, sin2, p)
        outs.append(h.reshape(batch, seq, D_MODEL))
    return tuple(outs)
```
